```python
import math
import jax, jax.numpy as jnp
from jax import lax
import numpy as np

D_MODEL = 4096
BATCH = 1
SEQ = 8192
DEPTH = 1
DEC_BATCH = 128
DEC_SEQ = 4
PAST_LEN = 8192
PAGE_SIZE = 128

MIX_WIDTH = D_MODEL
GLA_WIDTH = MIX_WIDTH // 2
SWA_WIDTH = MIX_WIDTH - GLA_WIDTH
GLA_HEADS = 4
GLA_DV = GLA_WIDTH // GLA_HEADS
GLA_DK = GLA_DV // 2
GLA_QK = GLA_HEADS * GLA_DK
GLA_LR = 16
GLA_TAU = 16.0
GLA_CHUNK = 64
HEAD_DIM = 128
SWA_HEADS = SWA_WIDTH // HEAD_DIM
SWA_KV_HEADS = 4
SWA_GROUP = SWA_HEADS // SWA_KV_HEADS
SWA_Q = SWA_HEADS * HEAD_DIM
SWA_KV = SWA_KV_HEADS * HEAD_DIM
WINDOW = 128
D_FF = 256 * ((8 * D_MODEL // 3 + 255) // 256)
CONV_W = 3
NORM_EPS = 1e-6
NEG_INF = -1e30
IN_SIZES = (GLA_QK, GLA_QK, GLA_WIDTH, GLA_WIDTH, GLA_LR, SWA_Q, SWA_KV, SWA_KV)
IN_WIDTH = sum(IN_SIZES)

kernel_name = "hymba_gla_swa_sink_convffn_adaln_step"


def rmsnorm(x, g):
    xf = x.astype(jnp.float32)
    y = xf * lax.rsqrt(jnp.mean(xf * xf, axis=-1, keepdims=True) + NORM_EPS)
    return (y * g.astype(jnp.float32)).astype(x.dtype)


def alibi_slopes():
    return jnp.exp2(-8.0 * jnp.arange(1, SWA_HEADS + 1, dtype=jnp.float32) / SWA_HEADS)


def gla_chunked(q, k, v, log_a, s0):
    B, L, H, dk = q.shape
    dv = v.shape[-1]
    C = math.gcd(L, GLA_CHUNK)
    NC = L // C
    f32 = jnp.float32
    q = q.astype(f32).reshape(B, NC, C, H, dk)
    k = k.astype(f32).reshape(B, NC, C, H, dk)
    v = v.astype(f32).reshape(B, NC, C, H, dv)
    b = jnp.cumsum(log_a.astype(f32).reshape(B, NC, C, H, dk), axis=2)
    qg = q * jnp.exp(b)
    kg = k * jnp.exp(-b)
    causal = jnp.tril(jnp.ones((C, C), dtype=bool))
    A = jnp.einsum('bnthk,bnshk->bnhts', qg, kg)
    A = jnp.where(causal, A, 0.0)
    o_intra = jnp.einsum('bnhts,bnshv->bnthv', A, v)
    b_last = b[:, :, -1]
    kd = k * jnp.exp(b_last[:, :, None] - b)
    dS = jnp.einsum('bnshk,bnshv->bnhkv', kd, v)
    decay = jnp.exp(b_last)

    def step(S, xs):
        qg_n, dS_n, dec_n = xs
        o_n = jnp.einsum('bthk,bhkv->bthv', qg_n, S)
        S = dec_n[..., None] * S + dS_n
        return S, o_n

    xs = (jnp.moveaxis(qg, 1, 0), jnp.moveaxis(dS, 1, 0), jnp.moveaxis(decay, 1, 0))
    s_final, o_inter = lax.scan(step, s0.astype(f32), xs)
    o = o_intra + jnp.moveaxis(o_inter, 0, 1)
    return o.reshape(B, L, H, dv), s_final.astype(s0.dtype)


def sink_attention(q, k, v, dist, valid, sinks):
    slopes = alibi_slopes().reshape(SWA_KV_HEADS, SWA_GROUP)
    s = jnp.einsum('bnqhgd,bnkhd->bnhgqk', q, k,
                   preferred_element_type=jnp.float32) * (HEAD_DIM ** -0.5)
    s = s - slopes[None, None, :, :, None, None] * dist[None, :, None, None]
    s = jnp.where(valid[None, :, None, None], s, NEG_INF)
    sink = jnp.broadcast_to(
        sinks.astype(jnp.float32).reshape(SWA_KV_HEADS, SWA_GROUP)[None, None, :, :, None, None],
        s.shape[:-1] + (1,))
    p = jax.nn.softmax(jnp.concatenate([s, sink], axis=-1), axis=-1)[..., :-1]
    return jnp.einsum('bnhgqk,bnkhd->bnqhgd', p.astype(v.dtype), v)


def swa_prompt(q, k, v, sinks):
    B, L = q.shape[:2]
    W = WINDOW
    NB = L // W
    qb = q.reshape(B, NB, W, SWA_KV_HEADS, SWA_GROUP, HEAD_DIM)
    pad = jnp.zeros((B, W, SWA_KV_HEADS, HEAD_DIM), k.dtype)
    def band(t):
        prev = jnp.concatenate([pad, t], axis=1)[:, :L].reshape(B, NB, W, SWA_KV_HEADS, HEAD_DIM)
        cur = t.reshape(B, NB, W, SWA_KV_HEADS, HEAD_DIM)
        return jnp.concatenate([prev, cur], axis=2)
    kb, vb = band(k), band(v)
    blk = jnp.arange(NB)[:, None, None]
    q_pos = blk * W + jnp.arange(W)[None, :, None]
    k_pos = (blk - 1) * W + jnp.arange(2 * W)[None, None, :]
    d = q_pos - k_pos
    valid = (d >= 0) & (d <= WINDOW) & (k_pos >= 0)
    o = sink_attention(qb, kb, vb, d.astype(jnp.float32), valid, sinks)
    return o.reshape(B, L, SWA_KV_HEADS, SWA_GROUP, HEAD_DIM)


def swa_sample(q, k, v, k_buf, v_buf, sinks):
    L = q.shape[1]
    wb = k_buf.shape[1]
    kc = jnp.concatenate([k_buf.astype(k.dtype), k], axis=1)
    vc = jnp.concatenate([v_buf.astype(v.dtype), v], axis=1)
    k_pos = PAST_LEN - wb + jnp.arange(wb + L)
    q_pos = PAST_LEN + jnp.arange(L)
    d = (q_pos[:, None] - k_pos[None, :])[None]
    valid = (d >= 0) & (d <= WINDOW) & (k_pos[None, None, :] >= 0)
    o = sink_attention(q[:, None], kc[:, None], vc[:, None], d.astype(jnp.float32), valid, sinks)
    return o[:, 0], kc[:, -wb:], vc[:, -wb:]


def trunk_layer(x, c, lp, s0, kv_buf, conv_buf):
    (w_ada, b_ada, g_norm, w_in, w_a_up, b_a, g_gla, swa_sinks,
     w_o, w_up, w_conv, b_conv, w_down) = lp
    B, L, _ = x.shape
    mod = jax.nn.silu(c) @ w_ada + b_ada
    sh1, sc1, gt1, sh2, sc2, gt2 = jnp.split(mod, 6, axis=-1)

    h = rmsnorm(x, g_norm[0]) * (1.0 + sc1[:, None]) + sh1[:, None]
    proj = h @ w_in
    offs = []
    acc = 0
    for sz in IN_SIZES[:-1]:
        acc += sz
        offs.append(acc)
    gq, gk, gv, gr, ga, sq, sk, sv = jnp.split(proj, offs, axis=-1)

    log_a = jax.nn.log_sigmoid((ga @ w_a_up + b_a).astype(jnp.float32)) / GLA_TAU
    o_gla, s_new = gla_chunked(
        gq.reshape(B, L, GLA_HEADS, GLA_DK) * (GLA_DK ** -0.5),
        gk.reshape(B, L, GLA_HEADS, GLA_DK),
        gv.reshape(B, L, GLA_HEADS, GLA_DV),
        log_a.reshape(B, L, GLA_HEADS, GLA_DK), s0)
    o_gla = rmsnorm(o_gla, g_gla) * jax.nn.silu(gr.reshape(B, L, GLA_HEADS, GLA_DV))

    q = sq.reshape(B, L, SWA_KV_HEADS, SWA_GROUP, HEAD_DIM)
    k = sk.reshape(B, L, SWA_KV_HEADS, HEAD_DIM)
    v = sv.reshape(B, L, SWA_KV_HEADS, HEAD_DIM)
    if kv_buf is None:
        o_swa = swa_prompt(q, k, v, swa_sinks)
        wp = min(WINDOW, L)
        kb_new, vb_new = k[:, L - wp:], v[:, L - wp:]
    else:
        o_swa, kb_new, vb_new = swa_sample(q, k, v, kv_buf[0], kv_buf[1], swa_sinks)

    mix = jnp.concatenate([o_gla.reshape(B, L, GLA_WIDTH).astype(x.dtype),
                           o_swa.reshape(B, L, SWA_WIDTH).astype(x.dtype)], axis=-1)
    x = x + gt1[:, None] * (mix @ w_o)

    h2 = rmsnorm(x, g_norm[1]) * (1.0 + sc2[:, None]) + sh2[:, None]
    u = h2 @ w_up
    ue = jnp.concatenate([conv_buf.astype(u.dtype), u], axis=1)
    uc = b_conv
    for j in range(CONV_W):
        uc = uc + w_conv[j] * ue[:, j:j + L]
    gate, val = jnp.split(uc, 2, axis=-1)
    x = x + gt2[:, None] * ((jax.nn.silu(gate) * val) @ w_down)
    conv_new = ue[:, -(CONV_W - 1):]
    return x, s_new, kb_new, vb_new, conv_new


def setup_inputs(seed: int = 0) -> dict:
    key = jax.random.key(seed)
    ks = jax.random.split(key, 32)
    nrm = lambda k, shape, s: jax.random.normal(k, shape, jnp.float32) * s
    wb = min(WINDOW, PAST_LEN)
    F2 = 2 * D_FF
    return {
        "x_prompt": nrm(ks[0], (BATCH, SEQ, D_MODEL), 1.0),
        "x_sample": nrm(ks[1], (DEC_BATCH, DEC_SEQ, D_MODEL), 1.0),
        "c_prompt": nrm(ks[2], (BATCH, D_MODEL), 1.0),
        "c_sample": nrm(ks[3], (DEC_BATCH, D_MODEL), 1.0),
        "state_gla": nrm(ks[4], (DEPTH, DEC_BATCH, GLA_HEADS, GLA_DK, GLA_DV), 0.5),
        "state_swa_k": nrm(ks[5], (DEPTH, DEC_BATCH, wb, SWA_KV_HEADS, HEAD_DIM), 1.0),
        "state_swa_v": nrm(ks[6], (DEPTH, DEC_BATCH, wb, SWA_KV_HEADS, HEAD_DIM), 1.0),
        "state_ffn_conv": nrm(ks[7], (DEPTH, DEC_BATCH, CONV_W - 1, F2), 1.0),
        "w_ada": nrm(ks[8], (DEPTH, D_MODEL, 6 * D_MODEL), D_MODEL ** -0.5),
        "b_ada": nrm(ks[9], (DEPTH, 6 * D_MODEL), 0.02),
        "g_norm": 1.0 + nrm(ks[10], (DEPTH, 2, D_MODEL), 0.02),
        "w_in": nrm(ks[11], (DEPTH, D_MODEL, IN_WIDTH), D_MODEL ** -0.5),
        "w_a_up": nrm(ks[12], (DEPTH, GLA_LR, GLA_QK), GLA_LR ** -0.5),
        "b_a": nrm(ks[13], (DEPTH, GLA_QK), 0.1),
        "g_gla": 1.0 + nrm(ks[14], (DEPTH, GLA_DV), 0.02),
        "swa_sinks": nrm(ks[15], (DEPTH, SWA_HEADS), 0.5),
        "w_o": nrm(ks[16], (DEPTH, MIX_WIDTH, D_MODEL), MIX_WIDTH ** -0.5),
        "w_up": nrm(ks[17], (DEPTH, D_MODEL, F2), D_MODEL ** -0.5),
        "w_conv": nrm(ks[18], (DEPTH, CONV_W, F2), CONV_W ** -0.5),
        "b_conv": nrm(ks[19], (DEPTH, F2), 0.02),
        "w_down": nrm(ks[20], (DEPTH, D_FF, D_MODEL), D_FF ** -0.5),
        "g_final": 1.0 + nrm(ks[21], (D_MODEL,), 0.02),
    }


def reference(x_prompt, x_sample, c_prompt, c_sample, state_gla, state_swa_k, state_swa_v,
              state_ffn_conv, w_ada, b_ada, g_norm, w_in, w_a_up, b_a, g_gla, swa_sinks,
              w_o, w_up, w_conv, b_conv, w_down, g_final):
    yp, ys = x_prompt, x_sample
    Bp = x_prompt.shape[0]
    gla_p, kp, vp, cp = [], [], [], []
    gla_s, kss, vss, cs = [], [], [], []
    for l in range(DEPTH):
        lp = (w_ada[l], b_ada[l], g_norm[l], w_in[l], w_a_up[l], b_a[l], g_gla[l],
              swa_sinks[l], w_o[l], w_up[l], w_conv[l], b_conv[l], w_down[l])
        s0_p = jnp.zeros((Bp, GLA_HEADS, GLA_DK, GLA_DV), state_gla.dtype)
        conv0_p = jnp.zeros((Bp, CONV_W - 1, 2 * D_FF), state_ffn_conv.dtype)
        yp, s_p, k_p, v_p, c_p = trunk_layer(yp, c_prompt, lp, s0_p, None, conv0_p)
        ys, s_s, k_s, v_s, c_s = trunk_layer(ys, c_sample, lp, state_gla[l],
                                             (state_swa_k[l], state_swa_v[l]), state_ffn_conv[l])
        gla_p.append(s_p); kp.append(k_p); vp.append(v_p); cp.append(c_p)
        gla_s.append(s_s); kss.append(k_s); vss.append(v_s); cs.append(c_s)
    y_prompt = rmsnorm(yp, g_final)
    y_sample = rmsnorm(ys, g_final)
    return (y_prompt, y_sample,
            jnp.stack(gla_p), jnp.stack(kp), jnp.stack(vp), jnp.stack(cp),
            jnp.stack(gla_s), jnp.stack(kss), jnp.stack(vss), jnp.stack(cs))
```

```python
import functools

import jax
import jax.numpy as jnp
from jax import lax
from jax.experimental import pallas as pl
from jax.experimental.pallas import tpu as pltpu

F32 = jnp.float32
BF16 = jnp.bfloat16

GLA_TAU = 16.0
GLA_CHUNK = 64
HEAD_DIM = 128
SWA_GROUP = 4
WINDOW = 128
NORM_EPS = 1e-6
NEG_INF = -1e30

LANES = 128
ROW_BLOCK = 128
VMEM_LIMIT_BYTES = 56 * 1024 * 1024


def _params(n_axes):
    return pltpu.CompilerParams(
        dimension_semantics=("arbitrary",) * n_axes,
        vmem_limit_bytes=VMEM_LIMIT_BYTES)


def _silu(x):
    return x * (1.0 / (1.0 + jnp.exp(-x)))


def _log_sigmoid(z):
    return jnp.minimum(z, 0.0) - jnp.log(1.0 + jnp.exp(-jnp.abs(z)))


def _rms(x, g):
    return x * lax.rsqrt(jnp.mean(x * x, axis=-1, keepdims=True) + NORM_EPS) * g


def _dot(a, b):
    return jnp.dot(a, b, preferred_element_type=F32)


def _dot_nt(a, b):
    return lax.dot_general(a, b, (((1,), (1,)), ((), ())), preferred_element_type=F32)


def _dot_exact_lhs(t_bf16, x):
    hi = x.astype(BF16)
    r1 = x - hi.astype(F32)
    mid = r1.astype(BF16)
    lo = (r1 - mid.astype(F32)).astype(BF16)
    return _dot(t_bf16, hi) + _dot(t_bf16, mid) + _dot(t_bf16, lo)


def _ada_body(c_ref, w_ref, b_ref, o_ref):
    a = _silu(c_ref[...]).astype(BF16)
    o_ref[...] = _dot(a, w_ref[...].astype(BF16)) + b_ref[...]


def _ada(c_all, w_ada, b_ada, tn=512):
    rows, d = c_all.shape
    n = w_ada.shape[1]
    return pl.pallas_call(
        _ada_body,
        grid=(n // tn,),
        in_specs=[pl.BlockSpec((rows, d), lambda j: (0, 0)),
                  pl.BlockSpec((d, tn), lambda j: (0, j)),
                  pl.BlockSpec((1, tn), lambda j: (0, j))],
        out_specs=pl.BlockSpec((rows, tn), lambda j: (0, j)),
        out_shape=jax.ShapeDtypeStruct((rows, n), F32),
        compiler_params=_params(1),
        name="ada_mod",
    )(c_all, w_ada, b_ada.reshape(1, n))


def _p_spec(width, n_p, col=0):
    return pl.BlockSpec((ROW_BLOCK, width), lambda i: (jnp.minimum(i, n_p - 1), col))


def _s_spec(width, n_p, col=0):
    return pl.BlockSpec((ROW_BLOCK, width), lambda i: (jnp.maximum(i - n_p, 0), col))


def _modp_spec(width, col):
    return pl.BlockSpec((8, width), lambda i: (0, col))


def _all_spec(width):
    return pl.BlockSpec((ROW_BLOCK, width), lambda i: (i, 0))


def _prenorm_body(xp, xs, scp, shp, scs, shs, g, o, *, n_p):
    i = pl.program_id(0)

    @pl.when(i < n_p)
    def _():
        h = _rms(xp[...], g[...]) * (1.0 + scp[0:1, :]) + shp[0:1, :]
        o[...] = h.astype(BF16)

    @pl.when(i >= n_p)
    def _():
        h = _rms(xs[...], g[...]) * (1.0 + scs[...]) + shs[...]
        o[...] = h.astype(BF16)


def _prenorm(x_p, x_s, mod_p, mod_s, g):
    p, d = x_p.shape
    m = p + x_s.shape[0]
    n_p = p // ROW_BLOCK
    return pl.pallas_call(
        functools.partial(_prenorm_body, n_p=n_p),
        grid=(m // ROW_BLOCK,),
        in_specs=[_p_spec(d, n_p), _s_spec(d, n_p),
                  _modp_spec(d, 1), _modp_spec(d, 0),
                  _s_spec(d, n_p, 1), _s_spec(d, n_p, 0),
                  pl.BlockSpec((1, d), lambda i: (0, 0))],
        out_specs=_all_spec(d),
        out_shape=jax.ShapeDtypeStruct((m, d), BF16),
        compiler_params=_params(1),
        name="prenorm1",
    )(x_p, x_s, mod_p, mod_p, mod_s, mod_s, g)


def _postattn_body(xp, xs, at, gtp, scp, shp, gts, scs, shs, g, x1, h2, *, n_p):
    i = pl.program_id(0)

    @pl.when(i < n_p)
    def _():
        x = xp[...] + gtp[0:1, :] * at[...]
        x1[...] = x
        h2[...] = (_rms(x, g[...]) * (1.0 + scp[0:1, :]) + shp[0:1, :]).astype(BF16)

    @pl.when(i >= n_p)
    def _():
        x = xs[...] + gts[...] * at[...]
        x1[...] = x
        h2[...] = (_rms(x, g[...]) * (1.0 + scs[...]) + shs[...]).astype(BF16)


def _postattn(x_p, x_s, attn, mod_p, mod_s, g):
    p, d = x_p.shape
    m = p + x_s.shape[0]
    n_p = p // ROW_BLOCK
    return pl.pallas_call(
        functools.partial(_postattn_body, n_p=n_p),
        grid=(m // ROW_BLOCK,),
        in_specs=[_p_spec(d, n_p), _s_spec(d, n_p), _all_spec(d),
                  _modp_spec(d, 2), _modp_spec(d, 4), _modp_spec(d, 3),
                  _s_spec(d, n_p, 2), _s_spec(d, n_p, 4), _s_spec(d, n_p, 3),
                  pl.BlockSpec((1, d), lambda i: (0, 0))],
        out_specs=[_all_spec(d), _all_spec(d)],
        out_shape=[jax.ShapeDtypeStruct((m, d), F32), jax.ShapeDtypeStruct((m, d), BF16)],
        compiler_params=_params(1),
        name="postattn",
    )(x_p, x_s, attn, mod_p, mod_p, mod_p, mod_s, mod_s, mod_s, g)


def _final_body(x1, dn, gtp, gts, g, yp, ys, *, n_p):
    i = pl.program_id(0)

    @pl.when(i < n_p)
    def _():
        yp[...] = _rms(x1[...] + gtp[0:1, :] * dn[...], g[...])

    @pl.when(i >= n_p)
    def _():
        ys[...] = _rms(x1[...] + gts[...] * dn[...], g[...])


def _final(x1, down, mod_p, mod_s, g, p):
    m, d = x1.shape
    n_p = p // ROW_BLOCK
    return pl.pallas_call(
        functools.partial(_final_body, n_p=n_p),
        grid=(m // ROW_BLOCK,),
        in_specs=[_all_spec(d), _all_spec(d), _modp_spec(d, 5), _s_spec(d, n_p, 5),
                  pl.BlockSpec((1, d), lambda i: (0, 0))],
        out_specs=[_p_spec(d, n_p), _s_spec(d, n_p)],
        out_shape=[jax.ShapeDtypeStruct((p, d), F32), jax.ShapeDtypeStruct((m - p, d), F32)],
        compiler_params=_params(1),
        name="final_norm",
    )(x1, down, mod_p, mod_s, g)


def _mm_body(*refs, k_splits):
    n_a = len(k_splits)
    a_refs, w_ref, o_ref, wbf = refs[:n_a], refs[n_a], refs[n_a + 1], refs[n_a + 2]

    @pl.when(pl.program_id(1) == 0)
    def _():
        wbf[...] = w_ref[...].astype(BF16)

    acc = None
    k0 = 0
    for a_ref, kk in zip(a_refs, k_splits):
        part = _dot(a_ref[...], wbf[k0:k0 + kk, :])
        acc = part if acc is None else acc + part
        k0 += kk
    o_ref[...] = acc.astype(o_ref.dtype)


def _mm(a_list, w, n_cols, tm, tn, out_dtype, name, col_block0=0):
    m = a_list[0].shape[0]
    k_splits = tuple(a.shape[1] for a in a_list)
    k = sum(k_splits)
    assert w.shape[0] == k and m % tm == 0 and n_cols % tn == 0
    return pl.pallas_call(
        functools.partial(_mm_body, k_splits=k_splits),
        grid=(n_cols // tn, m // tm),
        in_specs=[pl.BlockSpec((tm, kk), lambda j, i: (i, 0)) for kk in k_splits]
        + [pl.BlockSpec((k, tn), lambda j, i: (0, j + col_block0))],
        out_specs=pl.BlockSpec((tm, tn), lambda j, i: (i, j)),
        out_shape=jax.ShapeDtypeStruct((m, n_cols), out_dtype),
        scratch_shapes=[pltpu.VMEM((k, tn), BF16)],
        compiler_params=_params(2),
        name=name,
    )(*a_list, w)


def _gla_gates(ga, wa, ba):
    z = _dot(ga.astype(BF16), wa.astype(BF16)) + ba
    return _log_sigmoid(z) * (1.0 / GLA_TAU)


def _gla_out(o, gg, r):
    return (_rms(o, gg) * _silu(r)).astype(BF16)


def _gla_prompt_body(q_ref, k_ref, v_ref, r_ref, ga_ref, wa_ref, ba_ref, gg_ref,
                     o_ref, sf_ref, s_scr, *, heads, dk, dv):
    c = pl.program_id(0)
    ch = q_ref.shape[0]

    @pl.when(c == 0)
    def _():
        s_scr[...] = jnp.zeros_like(s_scr)

    la = _gla_gates(ga_ref[...], wa_ref[...], ba_ref[...])
    row = lax.broadcasted_iota(jnp.int32, (ch, ch), 0)
    col = lax.broadcasted_iota(jnp.int32, (ch, ch), 1)
    causal = row >= col
    b = _dot_exact_lhs(causal.astype(BF16), la)
    b_last = b[ch - 1:ch, :]
    scale = dk ** -0.5
    qg = (q_ref[...] * scale) * jnp.exp(b)
    kg = k_ref[...] * jnp.exp(-b)
    kd = k_ref[...] * jnp.exp(b_last - b)
    pad = jnp.zeros((LANES - ch, la.shape[1]), F32) if ch < LANES else None
    la_t = jnp.transpose(la if pad is None else jnp.concatenate([la, pad], axis=0))
    decay_col = jnp.exp(jnp.sum(la_t, axis=1, keepdims=True))
    v = v_ref[...]
    r = r_ref[...]
    for h in range(heads):
        ks, vs = slice(h * dk, (h + 1) * dk), slice(h * dv, (h + 1) * dv)
        qh = qg[:, ks].astype(BF16)
        vh = v[:, vs].astype(BF16)
        a = jnp.where(causal, _dot_nt(qh, kg[:, ks].astype(BF16)), 0.0)
        s0 = s_scr[h]
        o = _dot(a.astype(BF16), vh) + _dot(qh, s0.astype(BF16))
        kdh = kd[:, ks]
        if pad is not None:
            kdh = jnp.concatenate([kdh, pad[:, :dk]], axis=0)
            vh = jnp.concatenate([vh, jnp.zeros((LANES - ch, dv), BF16)], axis=0)
        s_scr[h] = decay_col[ks, :] * s0 + _dot(jnp.transpose(kdh).astype(BF16), vh)
        o_ref[:, vs] = _gla_out(o, gg_ref[...], r[:, vs])

    @pl.when(c == pl.num_programs(0) - 1)
    def _():
        sf_ref[...] = s_scr[...]


def _gla_prompt(proj_g, proj_s, wa, ba, gg, p, heads, dk, dv):
    m = proj_g.shape[0]
    ch = GLA_CHUNK
    qk, gw = heads * dk, heads * dv
    ga_col = (proj_s.shape[1] - LANES) // LANES
    return pl.pallas_call(
        functools.partial(_gla_prompt_body, heads=heads, dk=dk, dv=dv),
        grid=(p // ch,),
        in_specs=[pl.BlockSpec((ch, qk), lambda c: (c, 0)),
                  pl.BlockSpec((ch, qk), lambda c: (c, 1)),
                  pl.BlockSpec((ch, gw), lambda c: (c, 1)),
                  pl.BlockSpec((ch, gw), lambda c: (c, 2)),
                  pl.BlockSpec((ch, LANES), lambda c: (c, ga_col)),
                  pl.BlockSpec((LANES, qk), lambda c: (0, 0)),
                  pl.BlockSpec((1, qk), lambda c: (0, 0)),
                  pl.BlockSpec((1, dv), lambda c: (0, 0))],
        out_specs=[pl.BlockSpec((ch, gw), lambda c: (c, 0)),
                   pl.BlockSpec((heads, dk, dv), lambda c: (0, 0, 0))],
        out_shape=[jax.ShapeDtypeStruct((m, gw), BF16),
                   jax.ShapeDtypeStruct((heads, dk, dv), F32)],
        scratch_shapes=[pltpu.VMEM((heads, dk, dv), F32)],
        compiler_params=_params(1),
        name="gla_prompt",
    )(proj_g, proj_g, proj_g, proj_g, proj_s, wa, ba, gg)


GLA_S_TILE_BATCHES = ROW_BLOCK // 4
GLA_S_STEP_BATCHES = 2


def _gla_sample_body(q_ref, k_ref, v_ref, r_ref, ga_ref, wa_ref, ba_ref, gg_ref, st_ref, _o_in,
                     o_ref, ns_ref, qg_s, kdt_s, v_s, lat_s, o_s, *, heads, dk, dv):
    s = pl.program_id(1)
    n_sub = pl.num_programs(1)
    rows = ROW_BLOCK

    @pl.when(s == 0)
    def _():
        la = _gla_gates(ga_ref[...], wa_ref[...], ba_ref[...])
        row = lax.broadcasted_iota(jnp.int32, (rows, rows), 0)
        col = lax.broadcasted_iota(jnp.int32, (rows, rows), 1)
        same = (row // 4) == (col // 4)
        causal = same & (row >= col)
        b = _dot_exact_lhs(causal.astype(BF16), la)
        b_last = _dot_exact_lhs(same.astype(BF16), la)
        scale = dk ** -0.5
        qg = (q_ref[...] * scale) * jnp.exp(b)
        kg = k_ref[...] * jnp.exp(-b)
        kd = k_ref[...] * jnp.exp(b_last - b)
        lat_s[...] = jnp.transpose(la)
        v = v_ref[...].astype(BF16)
        v_s[...] = v
        qg_s[...] = qg.astype(BF16)
        for h in range(heads):
            ks, vs = slice(h * dk, (h + 1) * dk), slice(h * dv, (h + 1) * dv)
            a = jnp.where(causal, _dot_nt(qg[:, ks].astype(BF16), kg[:, ks].astype(BF16)), 0.0)
            o_s[:, vs] = _dot(a.astype(BF16), v[:, vs])
            kdt_s[h] = jnp.transpose(kd[:, ks]).astype(BF16)

    lane = lax.broadcasted_iota(jnp.int32, (1, rows), 1)
    sub = lax.broadcasted_iota(jnp.int32, (16, 1), 0)
    for i in range(GLA_S_STEP_BATCHES):
        bl = s * GLA_S_STEP_BATCHES + i
        lane_mask = (lane // 4) == bl
        decay_col = jnp.exp(jnp.sum(jnp.where(lane_mask, lat_s[...], 0.0), axis=1, keepdims=True))
        slab = pl.ds(pl.multiple_of((bl // 4) * 16, 16), 16)
        row_mask = (sub // 4) == (bl % 4)
        for h in range(heads):
            ks, vs = slice(h * dk, (h + 1) * dk), slice(h * dv, (h + 1) * dv)
            s0 = st_ref[i, h]
            oi = _dot(qg_s[slab, ks], s0.astype(BF16))
            o_s[slab, vs] = o_s[slab, vs] + jnp.where(row_mask, oi, 0.0)
            kdt = jnp.where(lane_mask, kdt_s[h], jnp.zeros_like(kdt_s[h]))
            ns_ref[i, h] = decay_col[ks, :] * s0 + _dot(kdt, v_s[:, vs])

    @pl.when(s == n_sub - 1)
    def _():
        r = r_ref[...]
        for h in range(heads):
            vs = slice(h * dv, (h + 1) * dv)
            o_ref[:, vs] = _gla_out(o_s[:, vs], gg_ref[...], r[:, vs])


def _gla_sample(proj_g, proj_s, wa, ba, gg, state, o_gla, p, heads, dk, dv):
    m = proj_g.shape[0]
    db = state.shape[0]
    qk, gw = heads * dk, heads * dv
    ga_col = (proj_s.shape[1] - LANES) // LANES
    pb = p // ROW_BLOCK
    n_tiles = db // GLA_S_TILE_BATCHES
    n_sub = GLA_S_TILE_BATCHES // GLA_S_STEP_BATCHES
    sb = GLA_S_STEP_BATCHES
    return pl.pallas_call(
        functools.partial(_gla_sample_body, heads=heads, dk=dk, dv=dv),
        grid=(n_tiles, n_sub),
        in_specs=[pl.BlockSpec((ROW_BLOCK, qk), lambda t, s: (pb + t, 0)),
                  pl.BlockSpec((ROW_BLOCK, qk), lambda t, s: (pb + t, 1)),
                  pl.BlockSpec((ROW_BLOCK, gw), lambda t, s: (pb + t, 1)),
                  pl.BlockSpec((ROW_BLOCK, gw), lambda t, s: (pb + t, 2)),
                  pl.BlockSpec((ROW_BLOCK, LANES), lambda t, s: (pb + t, ga_col)),
                  pl.BlockSpec((LANES, qk), lambda t, s: (0, 0)),
                  pl.BlockSpec((1, qk), lambda t, s: (0, 0)),
                  pl.BlockSpec((1, dv), lambda t, s: (0, 0)),
                  pl.BlockSpec((sb, heads, dk, dv), lambda t, s: (t * n_sub + s, 0, 0, 0)),
                  pl.BlockSpec(memory_space=pl.ANY)],
        out_specs=[pl.BlockSpec((ROW_BLOCK, gw), lambda t, s: (pb + t, 0)),
                   pl.BlockSpec((sb, heads, dk, dv), lambda t, s: (t * n_sub + s, 0, 0, 0))],
        out_shape=[jax.ShapeDtypeStruct((m, gw), BF16),
                   jax.ShapeDtypeStruct(state.shape, F32)],
        scratch_shapes=[pltpu.VMEM((ROW_BLOCK, qk), BF16),
                        pltpu.VMEM((heads, dk, ROW_BLOCK), BF16),
                        pltpu.VMEM((ROW_BLOCK, gw), BF16),
                        pltpu.VMEM((qk, ROW_BLOCK), F32),
                        pltpu.VMEM((ROW_BLOCK, gw), F32)],
        input_output_aliases={9: 0},
        compiler_params=_params(2),
        name="gla_sample",
    )(proj_g, proj_g, proj_g, proj_g, proj_s, wa, ba, gg, state, o_gla)


def _alibi_slope(head, n_heads):
    return 2.0 ** (-8.0 * (head + 1) / n_heads)


def _per_group(rows_per_group, values):
    n = SWA_GROUP * rows_per_group
    g = lax.broadcasted_iota(jnp.int32, (n, 1), 0) // rows_per_group
    out = jnp.full((n, 1), values[SWA_GROUP - 1], F32)
    for i in range(SWA_GROUP - 2, -1, -1):
        out = jnp.where(g == i, values[i], out)
    return out


def _sink_softmax_pv(s, valid, sink, vv):
    s = jnp.where(valid, s, NEG_INF)
    mx = jnp.maximum(jnp.max(s, axis=1, keepdims=True), sink)
    pr = jnp.exp(s - mx)
    den = jnp.sum(pr, axis=1, keepdims=True) + jnp.exp(sink - mx)
    return _dot(pr.astype(BF16), vv) / den


def _swa_prompt_body(sink_ref, q_ref, kp_ref, kc_ref, vp_ref, vc_ref, o_ref, *, kv_heads):
    n = pl.program_id(0)
    w = WINDOW
    n_heads = kv_heads * SWA_GROUP
    rows = SWA_GROUP * w
    qi = lax.broadcasted_iota(jnp.int32, (rows, 2 * w), 0) % w
    ci = lax.broadcasted_iota(jnp.int32, (rows, 2 * w), 1)
    dist = w + qi - ci
    valid = (dist >= 0) & (dist <= WINDOW) & ((n > 0) | (ci >= w))
    dist_f = dist.astype(F32)
    q = q_ref[...]
    for kh in range(kv_heads):
        hs = slice(kh * HEAD_DIM, (kh + 1) * HEAD_DIM)
        q4 = jnp.concatenate(
            [q[:, (kh * SWA_GROUP + g) * HEAD_DIM:(kh * SWA_GROUP + g + 1) * HEAD_DIM]
             for g in range(SWA_GROUP)], axis=0).astype(BF16)
        kk = jnp.concatenate([kp_ref[:, hs], kc_ref[:, hs]], axis=0).astype(BF16)
        vv = jnp.concatenate([vp_ref[:, hs], vc_ref[:, hs]], axis=0).astype(BF16)
        slope = _per_group(w, [_alibi_slope(kh * SWA_GROUP + g, n_heads) for g in range(SWA_GROUP)])
        sink = _per_group(w, [sink_ref[kh * SWA_GROUP + g] for g in range(SWA_GROUP)])
        s = _dot_nt(q4, kk) * (HEAD_DIM ** -0.5) - slope * dist_f
        o = _sink_softmax_pv(s, valid, sink, vv)
        for g in range(SWA_GROUP):
            c0 = (kh * SWA_GROUP + g) * HEAD_DIM
            o_ref[:, c0:c0 + HEAD_DIM] = o[g * w:(g + 1) * w].astype(BF16)


def _swa_prompt(proj_s, sinks, p, kv_heads):
    m = proj_s.shape[0]
    qw = kv_heads * SWA_GROUP * HEAD_DIM
    kvw = kv_heads * HEAD_DIM
    kcol, vcol = qw // kvw, qw // kvw + 1
    prev = lambda n: jnp.maximum(n - 1, 0)
    return pl.pallas_call(
        functools.partial(_swa_prompt_body, kv_heads=kv_heads),
        grid=(p // WINDOW,),
        in_specs=[pl.BlockSpec(memory_space=pltpu.SMEM),
                  pl.BlockSpec((WINDOW, qw), lambda n: (n, 0)),
                  pl.BlockSpec((WINDOW, kvw), lambda n: (prev(n), kcol)),
                  pl.BlockSpec((WINDOW, kvw), lambda n: (n, kcol)),
                  pl.BlockSpec((WINDOW, kvw), lambda n: (prev(n), vcol)),
                  pl.BlockSpec((WINDOW, kvw), lambda n: (n, vcol))],
        out_specs=pl.BlockSpec((WINDOW, qw), lambda n: (n, 0)),
        out_shape=jax.ShapeDtypeStruct((m, qw), BF16),
        compiler_params=_params(1),
        name="swa_prompt",
    )(sinks, proj_s, proj_s, proj_s, proj_s, proj_s)


SWA_S_STEP_BATCHES = 8


def _swa_sample_body(sink_ref, q_ref, kn_ref, vn_ref, kb_ref, vb_ref, _o_in,
                     o_ref, nk_ref, nv_ref, *, kv_heads):
    bb = SWA_S_STEP_BATCHES
    rb = 4 * bb
    w = WINDOW
    n_heads = kv_heads * SWA_GROUP
    rows = SWA_GROUP * rb
    keys = w + 8
    rr = lax.broadcasted_iota(jnp.int32, (rows, keys), 0) % rb
    ci = lax.broadcasted_iota(jnp.int32, (rows, keys), 1)
    t = rr % 4
    q = q_ref[...]
    kn = kn_ref[...]
    vn = vn_ref[...]
    row_batch = (lax.broadcasted_iota(jnp.int32, (rows, 1), 0) % rb) // 4
    for kh in range(kv_heads):
        hs = slice(kh * HEAD_DIM, (kh + 1) * HEAD_DIM)
        q4 = jnp.concatenate(
            [q[:, (kh * SWA_GROUP + g) * HEAD_DIM:(kh * SWA_GROUP + g + 1) * HEAD_DIM]
             for g in range(SWA_GROUP)], axis=0).astype(BF16)
        slope = _per_group(rb, [_alibi_slope(kh * SWA_GROUP + g, n_heads) for g in range(SWA_GROUP)])
        sink = _per_group(rb, [sink_ref[kh * SWA_GROUP + g] for g in range(SWA_GROUP)])
        acc = jnp.zeros((rows, HEAD_DIM), F32)
        for b in range(bb):
            g8 = 8 * (b // 2)
            kk = jnp.concatenate([kb_ref[b, :, hs], kn[g8:g8 + 8, hs]], axis=0).astype(BF16)
            vv = jnp.concatenate([vb_ref[b, :, hs], vn[g8:g8 + 8, hs]], axis=0).astype(BF16)
            jj = ci - w
            new_tok = jj % 4
            dist = jnp.where(ci < w, w + t - ci, t - new_tok)
            valid = ((ci < w) & (ci >= t)) | ((ci >= w) & ((jj // 4) == (b % 2)) & (new_tok <= t))
            s = _dot_nt(q4, kk) * (HEAD_DIM ** -0.5) - slope * dist.astype(F32)
            o = _sink_softmax_pv(s, valid, sink, vv)
            acc = jnp.where(row_batch == b, o, acc)
        for g in range(SWA_GROUP):
            c0 = (kh * SWA_GROUP + g) * HEAD_DIM
            o_ref[:, c0:c0 + HEAD_DIM] = acc[g * rb:(g + 1) * rb].astype(BF16)
    for b in range(bb):
        nk_ref[b, 0:w - 4, :] = kb_ref[b, 4:w, :]
        nk_ref[b, w - 4:w, :] = kn[4 * b:4 * b + 4, :]
        nv_ref[b, 0:w - 4, :] = vb_ref[b, 4:w, :]
        nv_ref[b, w - 4:w, :] = vn[4 * b:4 * b + 4, :]


def _swa_sample(proj_s, sinks, k_buf, v_buf, o_swa, p, kv_heads):
    m = proj_s.shape[0]
    db, wb, kvw = k_buf.shape
    assert wb == WINDOW
    qw = kv_heads * SWA_GROUP * HEAD_DIM
    kcol, vcol = qw // kvw, qw // kvw + 1
    bb = SWA_S_STEP_BATCHES
    rb = 4 * bb
    pb = p // rb
    return pl.pallas_call(
        functools.partial(_swa_sample_body, kv_heads=kv_heads),
        grid=(db // bb,),
        in_specs=[pl.BlockSpec(memory_space=pltpu.SMEM),
                  pl.BlockSpec((rb, qw), lambda i: (pb + i, 0)),
                  pl.BlockSpec((rb, kvw), lambda i: (pb + i, kcol)),
                  pl.BlockSpec((rb, kvw), lambda i: (pb + i, vcol)),
                  pl.BlockSpec((bb, wb, kvw), lambda i: (i, 0, 0)),
                  pl.BlockSpec((bb, wb, kvw), lambda i: (i, 0, 0)),
                  pl.BlockSpec(memory_space=pl.ANY)],
        out_specs=[pl.BlockSpec((rb, qw), lambda i: (pb + i, 0)),
                   pl.BlockSpec((bb, wb, kvw), lambda i: (i, 0, 0)),
                   pl.BlockSpec((bb, wb, kvw), lambda i: (i, 0, 0))],
        out_shape=[jax.ShapeDtypeStruct((m, qw), BF16),
                   jax.ShapeDtypeStruct(k_buf.shape, F32),
                   jax.ShapeDtypeStruct(v_buf.shape, F32)],
        input_output_aliases={6: 0},
        compiler_params=_params(1),
        name="swa_sample",
    )(sinks, proj_s, proj_s, proj_s, k_buf, v_buf, o_swa)


def _ffn_up_body(a_ref, wg_ref, wv_ref, cwg_ref, cwv_ref, cbg_ref, cbv_ref, pg_ref, pv_ref,
                 o_ref, cpg_ref, cpv_ref, usg_ref, usv_ref, wbf, carry):
    i = pl.program_id(1)
    n_i = pl.num_programs(1)
    tm, tn = o_ref.shape

    @pl.when(i == 0)
    def _():
        wbf[:, :tn] = wg_ref[...].astype(BF16)
        wbf[:, tn:] = wv_ref[...].astype(BF16)
        carry[...] = jnp.zeros_like(carry)

    u = _dot(a_ref[...], wbf[...])
    cw = jnp.concatenate([cwg_ref[...], cwv_ref[...]], axis=1)
    cb = jnp.concatenate([cbg_ref[...], cbv_ref[...]], axis=1)
    row = lax.broadcasted_iota(jnp.int32, (tm, 1), 0)

    def gated(prev2, prev1):
        uc = cb + cw[0:1] * prev2 + cw[1:2] * prev1 + cw[2:3] * u
        o_ref[...] = (_silu(uc[:, :tn]) * uc[:, tn:]).astype(BF16)

    @pl.when(i < n_i - 1)
    def _():
        prev1 = jnp.where(row == 0, carry[7:8, :], pltpu.roll(u, 1, 0))
        prev2 = jnp.where(row == 0, carry[6:7, :],
                          jnp.where(row == 1, carry[7:8, :], pltpu.roll(u, 2, 0)))
        gated(prev2, prev1)
        carry[...] = u[tm - 8:tm, :]

    @pl.when(i == n_i - 2)
    def _():
        cpg_ref[...] = u[tm - 8:tm, :tn]
        cpv_ref[...] = u[tm - 8:tm, tn:]

    @pl.when(i == n_i - 1)
    def _():
        patch = jnp.concatenate([pg_ref[...], pv_ref[...]], axis=1)
        tok = row % 4
        prev1 = jnp.where(tok == 0, pltpu.roll(patch, tm - 1, 0), pltpu.roll(u, 1, 0))
        prev2 = jnp.where(tok < 2, patch, pltpu.roll(u, 2, 0))
        gated(prev2, prev1)
        usg_ref[...] = u[:, :tn]
        usv_ref[...] = u[:, tn:]


def _ffn_up(h2, w_up, w_conv, b_conv, patch, p, tn=256):
    m, d = h2.shape
    f2 = w_up.shape[1]
    dff = f2 // 2
    tm = m - p
    assert p % tm == 0 and dff % tn == 0
    nj = dff // tn
    bc = b_conv.reshape(1, f2)
    gate = lambda j, i: (0, j)
    val = lambda j, i: (0, j + nj)
    return pl.pallas_call(
        _ffn_up_body,
        grid=(nj, m // tm),
        in_specs=[pl.BlockSpec((tm, d), lambda j, i: (i, 0)),
                  pl.BlockSpec((d, tn), gate), pl.BlockSpec((d, tn), val),
                  pl.BlockSpec((3, tn), gate), pl.BlockSpec((3, tn), val),
                  pl.BlockSpec((1, tn), gate), pl.BlockSpec((1, tn), val),
                  pl.BlockSpec((tm, tn), gate), pl.BlockSpec((tm, tn), val)],
        out_specs=[pl.BlockSpec((tm, tn), lambda j, i: (i, j)),
                   pl.BlockSpec((8, tn), gate), pl.BlockSpec((8, tn), gate),
                   pl.BlockSpec((tm, tn), gate), pl.BlockSpec((tm, tn), gate)],
        out_shape=[jax.ShapeDtypeStruct((m, dff), BF16),
                   jax.ShapeDtypeStruct((8, dff), F32), jax.ShapeDtypeStruct((8, dff), F32),
                   jax.ShapeDtypeStruct((tm, dff), F32), jax.ShapeDtypeStruct((tm, dff), F32)],
        scratch_shapes=[pltpu.VMEM((d, 2 * tn), BF16), pltpu.VMEM((8, 2 * tn), F32)],
        compiler_params=_params(2),
        name="ffn_up_conv_gate",
    )(h2, w_up, w_up, w_conv, w_conv, bc, bc, patch, patch)


def _pick_tile(m, candidates):
    for c in candidates:
        if m % c == 0:
            return c
    return m


def _layer(x_p, x_s, c_p, c_s, s_gla, k_buf, v_buf, conv_buf, w_ada, b_ada, g_norm, w_in,
           w_a_up, b_a, g_gla, sinks, w_o, w_up, w_conv, b_conv, w_down, g_final):
    p, d = x_p.shape
    db = c_s.shape[0]
    m = p + 4 * db
    heads, dk, dv = s_gla.shape[1:]
    qk, gw = heads * dk, heads * dv
    kv_heads = k_buf.shape[2]
    kvw = kv_heads * HEAD_DIM
    qw = kv_heads * SWA_GROUP * HEAD_DIM
    lr = w_a_up.shape[0]
    f2 = w_up.shape[1]

    c_all = jnp.concatenate([c_s, c_p, jnp.zeros((7, d), F32)], axis=0)
    mod = _ada(c_all, w_ada, b_ada)
    mod_p = mod[db:db + 8]
    mod_s = jnp.repeat(mod[:db], 4, axis=0)

    h = _prenorm(x_p, x_s, mod_p, mod_s, g_norm[0:1])

    tm = _pick_tile(m, (1088, 512, 384, 128))
    n_g = 2 * qk + 2 * gw
    proj_g = _mm([h], w_in, n_g, tm, 512, F32, "proj_gla")
    n_s = qw + 2 * kvw
    w_s = jnp.concatenate([w_in[:, n_g + lr:], w_in[:, n_g:n_g + lr],
                           jnp.zeros((d, LANES - lr), F32)], axis=1)
    proj_s = _mm([h], w_s, n_s + LANES, _pick_tile(m, (544, 384, 128)), 640, F32, "proj_swa")

    wa = jnp.concatenate([w_a_up, jnp.zeros((LANES - lr, qk), F32)], axis=0)
    ba = b_a.reshape(1, qk)
    gg = g_gla.reshape(1, dv)
    o_gla, s_p = _gla_prompt(proj_g, proj_s, wa, ba, gg, p, heads, dk, dv)
    o_gla, s_s = _gla_sample(proj_g, proj_s, wa, ba, gg, s_gla, o_gla, p, heads, dk, dv)

    o_swa = _swa_prompt(proj_s, sinks, p, kv_heads)
    o_swa, nk_s, nv_s = _swa_sample(proj_s, sinks, k_buf.reshape(db, WINDOW, kvw),
                                    v_buf.reshape(db, WINDOW, kvw), o_swa, p, kv_heads)
    nk_p = proj_s[p - WINDOW:p, qw:qw + kvw]
    nv_p = proj_s[p - WINDOW:p, qw + kvw:qw + 2 * kvw]

    attn = _mm([o_gla, o_swa], w_o, d, tm, 512, F32, "out_proj")
    x1, h2 = _postattn(x_p, x_s, attn, mod_p, mod_s, g_norm[1:2])

    patch = jnp.concatenate([conv_buf, jnp.zeros((db, 2, f2), F32)], axis=1).reshape(4 * db, f2)
    g_act, cpg, cpv, usg, usv = _ffn_up(h2, w_up, w_conv, b_conv, patch, p)
    conv_p = jnp.concatenate([cpg[6:8], cpv[6:8]], axis=1)
    conv_s = jnp.concatenate([usg, usv], axis=1).reshape(db, 4, f2)[:, 2:4]

    down = _mm([g_act], w_down, d, _pick_tile(m, (512, 384, 128)), 256, F32, "ffn_down")
    y_p, y_s = _final(x1, down, mod_p, mod_s, g_final.reshape(1, d), p)
    return (y_p, y_s, s_p, nk_p.reshape(WINDOW, kv_heads, HEAD_DIM),
            nv_p.reshape(WINDOW, kv_heads, HEAD_DIM), conv_p,
            s_s, nk_s.reshape(db, WINDOW, kv_heads, HEAD_DIM),
            nv_s.reshape(db, WINDOW, kv_heads, HEAD_DIM), conv_s)


def kernel(x_prompt, x_sample, c_prompt, c_sample, state_gla, state_swa_k, state_swa_v, state_ffn_conv, w_ada, b_ada, g_norm, w_in, w_a_up, b_a, g_gla, swa_sinks, w_o, w_up, w_conv, b_conv, w_down, g_final):
    assert x_prompt.shape[0] == 1 and w_ada.shape[0] == 1
    db, ds, d = x_sample.shape
    assert ds == 4
    outs = _layer(x_prompt[0], x_sample.reshape(db * ds, d), c_prompt, c_sample,
                  state_gla[0], state_swa_k[0], state_swa_v[0], state_ffn_conv[0],
                  w_ada[0], b_ada[0], g_norm[0], w_in[0], w_a_up[0], b_a[0], g_gla[0],
                  swa_sinks[0], w_o[0], w_up[0], w_conv[0], b_conv[0], w_down[0], g_final)
    y_p, y_s, s_p, nk_p, nv_p, conv_p, s_s, nk_s, nv_s, conv_s = outs
    return (y_p[None], y_s.reshape(db, ds, d), s_p[None, None], nk_p[None, None], nv_p[None, None],
            conv_p[None, None], s_s[None], nk_s[None], nv_s[None], conv_s[None])
```

```python
import functools

import jax
import jax.numpy as jnp
from jax import lax
from jax.experimental import pallas as pl
from jax.experimental.pallas import tpu as pltpu

F32 = jnp.float32
BF16 = jnp.bfloat16

GLA_TAU = 16.0
GLA_CHUNK = 64
HEAD_DIM = 128
SWA_GROUP = 4
WINDOW = 128
NORM_EPS = 1e-6
NEG_INF = -1e30

LANES = 128
ROW_BLOCK = 128
VMEM_LIMIT_BYTES = 56 * 1024 * 1024


def _params(n_axes):
    return pltpu.CompilerParams(
        dimension_semantics=("arbitrary",) * n_axes,
        vmem_limit_bytes=VMEM_LIMIT_BYTES)


def _silu(x):
    return x * (1.0 / (1.0 + jnp.exp(-x)))


def _log_sigmoid(z):
    return jnp.minimum(z, 0.0) - jnp.log(1.0 + jnp.exp(-jnp.abs(z)))


def _rms(x, g):
    return x * lax.rsqrt(jnp.mean(x * x, axis=-1, keepdims=True) + NORM_EPS) * g


def _dot(a, b):
    return jnp.dot(a, b, preferred_element_type=F32)


def _dot_nt(a, b):
    return lax.dot_general(a, b, (((1,), (1,)), ((), ())), preferred_element_type=F32)


def _dot_exact_lhs(t_bf16, x):
    hi = x.astype(BF16)
    r1 = x - hi.astype(F32)
    mid = r1.astype(BF16)
    lo = (r1 - mid.astype(F32)).astype(BF16)
    return _dot(t_bf16, hi) + _dot(t_bf16, mid) + _dot(t_bf16, lo)


def _ada_body(c_ref, w_ref, b_ref, o_ref):
    a = _silu(c_ref[...]).astype(BF16)
    o_ref[...] = _dot(a, w_ref[...].astype(BF16)) + b_ref[...]


def _ada(c_all, w_ada, b_ada, tn=512):
    rows, d = c_all.shape
    n = w_ada.shape[1]
    return pl.pallas_call(
        _ada_body,
        grid=(n // tn,),
        in_specs=[pl.BlockSpec((rows, d), lambda j: (0, 0)),
                  pl.BlockSpec((d, tn), lambda j: (0, j)),
                  pl.BlockSpec((1, tn), lambda j: (0, j))],
        out_specs=pl.BlockSpec((rows, tn), lambda j: (0, j)),
        out_shape=jax.ShapeDtypeStruct((rows, n), F32),
        compiler_params=_params(1),
        name="ada_mod",
    )(c_all, w_ada, b_ada.reshape(1, n))


def _p_spec(width, n_p, col=0):
    return pl.BlockSpec((ROW_BLOCK, width), lambda i: (jnp.minimum(i, n_p - 1), col))


def _s_spec(width, n_p, col=0):
    return pl.BlockSpec((ROW_BLOCK, width), lambda i: (jnp.maximum(i - n_p, 0), col))


def _modp_spec(width, col, prompt_row_block):
    return pl.BlockSpec((8, width), lambda i: (prompt_row_block, col))


def _mods_spec(width, n_p, col):
    return pl.BlockSpec((ROW_BLOCK // 4, width), lambda i: (jnp.maximum(i - n_p, 0), col))


def _all_spec(width):
    return pl.BlockSpec((ROW_BLOCK, width), lambda i: (i, 0))


def _per_token(ref):
    v = ref[...]
    n = v.shape[0]
    tok_batch = lax.broadcasted_iota(jnp.int32, (4 * n, n), 0) // 4
    batch = lax.broadcasted_iota(jnp.int32, (4 * n, n), 1)
    return _dot_exact_lhs((tok_batch == batch).astype(BF16), v)


def _prenorm_body(xp, xs, scp, shp, scs, shs, g, o, *, n_p):
    i = pl.program_id(0)

    @pl.when(i < n_p)
    def _():
        h = _rms(xp[...], g[...]) * (1.0 + scp[0:1, :]) + shp[0:1, :]
        o[...] = h.astype(BF16)

    @pl.when(i >= n_p)
    def _():
        h = _rms(xs[...], g[...]) * (1.0 + _per_token(scs)) + _per_token(shs)
        o[...] = h.astype(BF16)


def _prenorm(x_p, x_s, mod, g):
    p, d = x_p.shape
    m = p + x_s.shape[0]
    n_p = p // ROW_BLOCK
    prow = (m - p) // 4 // 8
    return pl.pallas_call(
        functools.partial(_prenorm_body, n_p=n_p),
        grid=(m // ROW_BLOCK,),
        in_specs=[_p_spec(d, n_p), _s_spec(d, n_p),
                  _modp_spec(d, 1, prow), _modp_spec(d, 0, prow),
                  _mods_spec(d, n_p, 1), _mods_spec(d, n_p, 0),
                  pl.BlockSpec((1, d), lambda i: (0, 0))],
        out_specs=_all_spec(d),
        out_shape=jax.ShapeDtypeStruct((m, d), BF16),
        compiler_params=_params(1),
        name="prenorm1",
    )(x_p, x_s, mod, mod, mod, mod, g)


def _postattn_body(xp, xs, at, gtp, scp, shp, gts, scs, shs, g, x1, h2, *, n_p):
    i = pl.program_id(0)

    @pl.when(i < n_p)
    def _():
        x = xp[...] + gtp[0:1, :] * at[...]
        x1[...] = x
        h2[...] = (_rms(x, g[...]) * (1.0 + scp[0:1, :]) + shp[0:1, :]).astype(BF16)

    @pl.when(i >= n_p)
    def _():
        x = xs[...] + _per_token(gts) * at[...]
        x1[...] = x
        h2[...] = (_rms(x, g[...]) * (1.0 + _per_token(scs)) + _per_token(shs)).astype(BF16)


def _postattn(x_p, x_s, attn, mod, g):
    p, d = x_p.shape
    m = p + x_s.shape[0]
    n_p = p // ROW_BLOCK
    prow = (m - p) // 4 // 8
    return pl.pallas_call(
        functools.partial(_postattn_body, n_p=n_p),
        grid=(m // ROW_BLOCK,),
        in_specs=[_p_spec(d, n_p), _s_spec(d, n_p), _all_spec(d),
                  _modp_spec(d, 2, prow), _modp_spec(d, 4, prow), _modp_spec(d, 3, prow),
                  _mods_spec(d, n_p, 2), _mods_spec(d, n_p, 4), _mods_spec(d, n_p, 3),
                  pl.BlockSpec((1, d), lambda i: (0, 0))],
        out_specs=[_all_spec(d), _all_spec(d)],
        out_shape=[jax.ShapeDtypeStruct((m, d), F32), jax.ShapeDtypeStruct((m, d), BF16)],
        compiler_params=_params(1),
        name="postattn",
    )(x_p, x_s, attn, mod, mod, mod, mod, mod, mod, g)


def _final_body(x1, dn, gtp, gts, g, yp, ys, *, n_p):
    i = pl.program_id(0)

    @pl.when(i < n_p)
    def _():
        yp[...] = _rms(x1[...] + gtp[0:1, :] * dn[...], g[...])

    @pl.when(i >= n_p)
    def _():
        ys[...] = _rms(x1[...] + _per_token(gts) * dn[...], g[...])


def _final(x1, down, mod, g, p):
    m, d = x1.shape
    n_p = p // ROW_BLOCK
    prow = (m - p) // 4 // 8
    return pl.pallas_call(
        functools.partial(_final_body, n_p=n_p),
        grid=(m // ROW_BLOCK,),
        in_specs=[_all_spec(d), _all_spec(d), _modp_spec(d, 5, prow), _mods_spec(d, n_p, 5),
                  pl.BlockSpec((1, d), lambda i: (0, 0))],
        out_specs=[_p_spec(d, n_p), _s_spec(d, n_p)],
        out_shape=[jax.ShapeDtypeStruct((p, d), F32), jax.ShapeDtypeStruct((m - p, d), F32)],
        compiler_params=_params(1),
        name="final_norm",
    )(x1, down, mod, mod, g)


def _mm_body(*refs, k_splits):
    n_a = len(k_splits)
    a_refs, w_ref, o_ref, wbf = refs[:n_a], refs[n_a], refs[n_a + 1], refs[n_a + 2]

    @pl.when(pl.program_id(1) == 0)
    def _():
        wbf[...] = w_ref[...].astype(BF16)

    acc = None
    k0 = 0
    for a_ref, kk in zip(a_refs, k_splits):
        part = _dot(a_ref[...], wbf[k0:k0 + kk, :])
        acc = part if acc is None else acc + part
        k0 += kk
    o_ref[...] = acc.astype(o_ref.dtype)


def _mm(a_list, w, n_cols, tm, tn, out_dtype, name, col_block0=0):
    m = a_list[0].shape[0]
    k_splits = tuple(a.shape[1] for a in a_list)
    k = sum(k_splits)
    assert w.shape[0] == k and m % tm == 0 and n_cols % tn == 0
    return pl.pallas_call(
        functools.partial(_mm_body, k_splits=k_splits),
        grid=(n_cols // tn, m // tm),
        in_specs=[pl.BlockSpec((tm, kk), lambda j, i: (i, 0)) for kk in k_splits]
        + [pl.BlockSpec((k, tn), lambda j, i: (0, j + col_block0))],
        out_specs=pl.BlockSpec((tm, tn), lambda j, i: (i, j)),
        out_shape=jax.ShapeDtypeStruct((m, n_cols), out_dtype),
        scratch_shapes=[pltpu.VMEM((k, tn), BF16)],
        compiler_params=_params(2),
        name=name,
    )(*a_list, w)


def _mm_nt_body(a_ref, w_ref, o_ref, wbf):
    @pl.when(pl.program_id(1) == 0)
    def _():
        wbf[...] = w_ref[...].astype(BF16)

    o_ref[...] = _dot_nt(a_ref[...], wbf[...]).astype(o_ref.dtype)


def _mm_nt(a, w_t, row_offset, n_tiles, tm, tn, out_dtype, name):
    m, k = a.shape
    assert w_t.shape[1] == k and m % tm == 0
    return pl.pallas_call(
        _mm_nt_body,
        grid=(n_tiles, m // tm),
        in_specs=[pl.BlockSpec((tm, k), lambda j, i: (i, 0)),
                  pl.BlockSpec((pl.Element(tn), pl.Element(k)),
                               lambda j, i: (pl.multiple_of(row_offset(j), 8), 0))],
        out_specs=pl.BlockSpec((tm, tn), lambda j, i: (i, j)),
        out_shape=jax.ShapeDtypeStruct((m, n_tiles * tn), out_dtype),
        scratch_shapes=[pltpu.VMEM((tn, k), BF16)],
        compiler_params=_params(2),
        name=name,
    )(a, w_t)


def _gla_gates(ga, wa, ba):
    z = _dot(ga.astype(BF16), wa.astype(BF16)) + ba
    return _log_sigmoid(z) * (1.0 / GLA_TAU)


def _gla_out(o, gg, r):
    return (_rms(o, gg) * _silu(r)).astype(BF16)


def _gla_prompt_body(q_ref, k_ref, v_ref, r_ref, ga_ref, wa_ref, ba_ref, gg_ref,
                     o_ref, sf_ref, s_scr, *, heads, dk, dv):
    c = pl.program_id(0)
    ch = q_ref.shape[0]

    @pl.when(c == 0)
    def _():
        s_scr[...] = jnp.zeros_like(s_scr)

    la = _gla_gates(ga_ref[...], wa_ref[...], ba_ref[...])
    row = lax.broadcasted_iota(jnp.int32, (ch, ch), 0)
    col = lax.broadcasted_iota(jnp.int32, (ch, ch), 1)
    causal = row >= col
    b = _dot_exact_lhs(causal.astype(BF16), la)
    b_last = b[ch - 1:ch, :]
    scale = dk ** -0.5
    qg = (q_ref[...] * scale) * jnp.exp(b)
    kg = k_ref[...] * jnp.exp(-b)
    kd = k_ref[...] * jnp.exp(b_last - b)
    pad = jnp.zeros((LANES - ch, la.shape[1]), F32) if ch < LANES else None
    la_t = jnp.transpose(la if pad is None else jnp.concatenate([la, pad], axis=0))
    decay_col = jnp.exp(jnp.sum(la_t, axis=1, keepdims=True))
    v = v_ref[...]
    r = r_ref[...]
    for h in range(heads):
        ks, vs = slice(h * dk, (h + 1) * dk), slice(h * dv, (h + 1) * dv)
        qh = qg[:, ks].astype(BF16)
        vh = v[:, vs].astype(BF16)
        a = jnp.where(causal, _dot_nt(qh, kg[:, ks].astype(BF16)), 0.0)
        s0 = s_scr[h]
        o = _dot(a.astype(BF16), vh) + _dot(qh, s0.astype(BF16))
        kdh = kd[:, ks]
        if pad is not None:
            kdh = jnp.concatenate([kdh, pad[:, :dk]], axis=0)
            vh = jnp.concatenate([vh, jnp.zeros((LANES - ch, dv), BF16)], axis=0)
        s_scr[h] = decay_col[ks, :] * s0 + _dot(jnp.transpose(kdh).astype(BF16), vh)
        o_ref[:, vs] = _gla_out(o, gg_ref[...], r[:, vs])

    @pl.when(c == pl.num_programs(0) - 1)
    def _():
        sf_ref[...] = s_scr[...]


def _gla_prompt(proj_g, proj_s, wa, ba, gg, p, heads, dk, dv):
    m = proj_g.shape[0]
    ch = GLA_CHUNK
    qk, gw = heads * dk, heads * dv
    ga_col = (proj_s.shape[1] - LANES) // LANES
    return pl.pallas_call(
        functools.partial(_gla_prompt_body, heads=heads, dk=dk, dv=dv),
        grid=(p // ch,),
        in_specs=[pl.BlockSpec((ch, qk), lambda c: (c, 0)),
                  pl.BlockSpec((ch, qk), lambda c: (c, 1)),
                  pl.BlockSpec((ch, gw), lambda c: (c, 1)),
                  pl.BlockSpec((ch, gw), lambda c: (c, 2)),
                  pl.BlockSpec((ch, LANES), lambda c: (c, ga_col)),
                  pl.BlockSpec((LANES, qk), lambda c: (0, 0)),
                  pl.BlockSpec((1, qk), lambda c: (0, 0)),
                  pl.BlockSpec((1, dv), lambda c: (0, 0))],
        out_specs=[pl.BlockSpec((ch, gw), lambda c: (c, 0)),
                   pl.BlockSpec((heads, dk, dv), lambda c: (0, 0, 0))],
        out_shape=[jax.ShapeDtypeStruct((m, gw), BF16),
                   jax.ShapeDtypeStruct((heads, dk, dv), F32)],
        scratch_shapes=[pltpu.VMEM((heads, dk, dv), F32)],
        compiler_params=_params(1),
        name="gla_prompt",
    )(proj_g, proj_g, proj_g, proj_g, proj_s, wa, ba, gg)


GLA_S_TILE_BATCHES = ROW_BLOCK // 4
GLA_S_STEP_BATCHES = 2


def _gla_sample_body(q_ref, k_ref, v_ref, r_ref, ga_ref, wa_ref, ba_ref, gg_ref, st_ref, _o_in,
                     o_ref, ns_ref, qg_s, kdt_s, v_s, lat_s, o_s, *, heads, dk, dv):
    s = pl.program_id(1)
    n_sub = pl.num_programs(1)
    rows = ROW_BLOCK

    @pl.when(s == 0)
    def _():
        la = _gla_gates(ga_ref[...], wa_ref[...], ba_ref[...])
        row = lax.broadcasted_iota(jnp.int32, (rows, rows), 0)
        col = lax.broadcasted_iota(jnp.int32, (rows, rows), 1)
        same = (row // 4) == (col // 4)
        causal = same & (row >= col)
        b = _dot_exact_lhs(causal.astype(BF16), la)
        b_last = _dot_exact_lhs(same.astype(BF16), la)
        scale = dk ** -0.5
        qg = (q_ref[...] * scale) * jnp.exp(b)
        kg = k_ref[...] * jnp.exp(-b)
        kd = k_ref[...] * jnp.exp(b_last - b)
        lat_s[...] = jnp.transpose(la)
        v = v_ref[...].astype(BF16)
        v_s[...] = v
        qg_s[...] = qg.astype(BF16)
        for h in range(heads):
            ks, vs = slice(h * dk, (h + 1) * dk), slice(h * dv, (h + 1) * dv)
            a = jnp.where(causal, _dot_nt(qg[:, ks].astype(BF16), kg[:, ks].astype(BF16)), 0.0)
            o_s[:, vs] = _dot(a.astype(BF16), v[:, vs])
            kdt_s[h] = jnp.transpose(kd[:, ks]).astype(BF16)

    lane = lax.broadcasted_iota(jnp.int32, (1, rows), 1)
    sub = lax.broadcasted_iota(jnp.int32, (16, 1), 0)
    for i in range(GLA_S_STEP_BATCHES):
        bl = s * GLA_S_STEP_BATCHES + i
        lane_mask = (lane // 4) == bl
        decay_col = jnp.exp(jnp.sum(jnp.where(lane_mask, lat_s[...], 0.0), axis=1, keepdims=True))
        slab = pl.ds(pl.multiple_of((bl // 4) * 16, 16), 16)
        row_mask = (sub // 4) == (bl % 4)
        for h in range(heads):
            ks, vs = slice(h * dk, (h + 1) * dk), slice(h * dv, (h + 1) * dv)
            s0 = st_ref[i, h]
            oi = _dot(qg_s[slab, ks], s0.astype(BF16))
            o_s[slab, vs] = o_s[slab, vs] + jnp.where(row_mask, oi, 0.0)
            kdt = jnp.where(lane_mask, kdt_s[h], jnp.zeros_like(kdt_s[h]))
            ns_ref[i, h] = decay_col[ks, :] * s0 + _dot(kdt, v_s[:, vs])

    @pl.when(s == n_sub - 1)
    def _():
        r = r_ref[...]
        for h in range(heads):
            vs = slice(h * dv, (h + 1) * dv)
            o_ref[:, vs] = _gla_out(o_s[:, vs], gg_ref[...], r[:, vs])


def _gla_sample(proj_g, proj_s, wa, ba, gg, state, o_gla, p, heads, dk, dv):
    m = proj_g.shape[0]
    db = state.shape[0]
    qk, gw = heads * dk, heads * dv
    ga_col = (proj_s.shape[1] - LANES) // LANES
    pb = p // ROW_BLOCK
    n_tiles = db // GLA_S_TILE_BATCHES
    n_sub = GLA_S_TILE_BATCHES // GLA_S_STEP_BATCHES
    sb = GLA_S_STEP_BATCHES
    return pl.pallas_call(
        functools.partial(_gla_sample_body, heads=heads, dk=dk, dv=dv),
        grid=(n_tiles, n_sub),
        in_specs=[pl.BlockSpec((ROW_BLOCK, qk), lambda t, s: (pb + t, 0)),
                  pl.BlockSpec((ROW_BLOCK, qk), lambda t, s: (pb + t, 1)),
                  pl.BlockSpec((ROW_BLOCK, gw), lambda t, s: (pb + t, 1)),
                  pl.BlockSpec((ROW_BLOCK, gw), lambda t, s: (pb + t, 2)),
                  pl.BlockSpec((ROW_BLOCK, LANES), lambda t, s: (pb + t, ga_col)),
                  pl.BlockSpec((LANES, qk), lambda t, s: (0, 0)),
                  pl.BlockSpec((1, qk), lambda t, s: (0, 0)),
                  pl.BlockSpec((1, dv), lambda t, s: (0, 0)),
                  pl.BlockSpec((sb, heads, dk, dv), lambda t, s: (t * n_sub + s, 0, 0, 0)),
                  pl.BlockSpec(memory_space=pl.ANY)],
        out_specs=[pl.BlockSpec((ROW_BLOCK, gw), lambda t, s: (pb + t, 0)),
                   pl.BlockSpec((sb, heads, dk, dv), lambda t, s: (t * n_sub + s, 0, 0, 0))],
        out_shape=[jax.ShapeDtypeStruct((m, gw), BF16),
                   jax.ShapeDtypeStruct(state.shape, F32)],
        scratch_shapes=[pltpu.VMEM((ROW_BLOCK, qk), BF16),
                        pltpu.VMEM((heads, dk, ROW_BLOCK), BF16),
                        pltpu.VMEM((ROW_BLOCK, gw), BF16),
                        pltpu.VMEM((qk, ROW_BLOCK), F32),
                        pltpu.VMEM((ROW_BLOCK, gw), F32)],
        input_output_aliases={9: 0},
        compiler_params=_params(2),
        name="gla_sample",
    )(proj_g, proj_g, proj_g, proj_g, proj_s, wa, ba, gg, state, o_gla)


def _alibi_slope(head, n_heads):
    return 2.0 ** (-8.0 * (head + 1) / n_heads)


def _per_group(rows_per_group, values):
    n = SWA_GROUP * rows_per_group
    g = lax.broadcasted_iota(jnp.int32, (n, 1), 0) // rows_per_group
    out = jnp.full((n, 1), values[SWA_GROUP - 1], F32)
    for i in range(SWA_GROUP - 2, -1, -1):
        out = jnp.where(g == i, values[i], out)
    return out


def _sink_softmax_pv(s, valid, sink, vv):
    s = jnp.where(valid, s, NEG_INF)
    mx = jnp.maximum(jnp.max(s, axis=1, keepdims=True), sink)
    pr = jnp.exp(s - mx)
    den = jnp.sum(pr, axis=1, keepdims=True) + jnp.exp(sink - mx)
    return _dot(pr.astype(BF16), vv) / den


def _swa_prompt_body(sink_ref, q_ref, kp_ref, kc_ref, vp_ref, vc_ref, o_ref, *, kv_heads):
    n = pl.program_id(0)
    w = WINDOW
    n_heads = kv_heads * SWA_GROUP
    rows = SWA_GROUP * w
    qi = lax.broadcasted_iota(jnp.int32, (rows, 2 * w), 0) % w
    ci = lax.broadcasted_iota(jnp.int32, (rows, 2 * w), 1)
    dist = w + qi - ci
    valid = (dist >= 0) & (dist <= WINDOW) & ((n > 0) | (ci >= w))
    dist_f = dist.astype(F32)
    q = q_ref[...]
    for kh in range(kv_heads):
        hs = slice(kh * HEAD_DIM, (kh + 1) * HEAD_DIM)
        q4 = jnp.concatenate(
            [q[:, (kh * SWA_GROUP + g) * HEAD_DIM:(kh * SWA_GROUP + g + 1) * HEAD_DIM]
             for g in range(SWA_GROUP)], axis=0).astype(BF16)
        kk = jnp.concatenate([kp_ref[:, hs], kc_ref[:, hs]], axis=0).astype(BF16)
        vv = jnp.concatenate([vp_ref[:, hs], vc_ref[:, hs]], axis=0).astype(BF16)
        slope = _per_group(w, [_alibi_slope(kh * SWA_GROUP + g, n_heads) for g in range(SWA_GROUP)])
        sink = _per_group(w, [sink_ref[kh * SWA_GROUP + g] for g in range(SWA_GROUP)])
        s = _dot_nt(q4, kk) * (HEAD_DIM ** -0.5) - slope * dist_f
        o = _sink_softmax_pv(s, valid, sink, vv)
        for g in range(SWA_GROUP):
            c0 = (kh * SWA_GROUP + g) * HEAD_DIM
            o_ref[:, c0:c0 + HEAD_DIM] = o[g * w:(g + 1) * w].astype(BF16)


def _swa_cols(col0, qw, kvw):
    assert col0 % qw == 0 and (col0 + qw) % kvw == 0
    return col0 // qw, (col0 + qw) // kvw, (col0 + qw) // kvw + 1


def _swa_prompt(proj_s, col0, sinks, p, kv_heads):
    m = proj_s.shape[0]
    qw = kv_heads * SWA_GROUP * HEAD_DIM
    kvw = kv_heads * HEAD_DIM
    qcol, kcol, vcol = _swa_cols(col0, qw, kvw)
    prev = lambda n: jnp.maximum(n - 1, 0)
    return pl.pallas_call(
        functools.partial(_swa_prompt_body, kv_heads=kv_heads),
        grid=(p // WINDOW,),
        in_specs=[pl.BlockSpec(memory_space=pltpu.SMEM),
                  pl.BlockSpec((WINDOW, qw), lambda n: (n, qcol)),
                  pl.BlockSpec((WINDOW, kvw), lambda n: (prev(n), kcol)),
                  pl.BlockSpec((WINDOW, kvw), lambda n: (n, kcol)),
                  pl.BlockSpec((WINDOW, kvw), lambda n: (prev(n), vcol)),
                  pl.BlockSpec((WINDOW, kvw), lambda n: (n, vcol))],
        out_specs=pl.BlockSpec((WINDOW, qw), lambda n: (n, 0)),
        out_shape=jax.ShapeDtypeStruct((m, qw), BF16),
        compiler_params=_params(1),
        name="swa_prompt",
    )(sinks, proj_s, proj_s, proj_s, proj_s, proj_s)


SWA_S_STEP_BATCHES = 8


def _swa_sample_body(sink_ref, q_ref, kn_ref, vn_ref, kb_ref, vb_ref, _o_in,
                     o_ref, nk_ref, nv_ref, *, kv_heads):
    bb = SWA_S_STEP_BATCHES
    rb = 4 * bb
    w = WINDOW
    n_heads = kv_heads * SWA_GROUP
    rows = SWA_GROUP * rb
    keys = w + 8
    rr = lax.broadcasted_iota(jnp.int32, (rows, keys), 0) % rb
    ci = lax.broadcasted_iota(jnp.int32, (rows, keys), 1)
    t = rr % 4
    q = q_ref[...]
    kn = kn_ref[...]
    vn = vn_ref[...]
    row_batch = (lax.broadcasted_iota(jnp.int32, (rows, 1), 0) % rb) // 4
    for kh in range(kv_heads):
        hs = slice(kh * HEAD_DIM, (kh + 1) * HEAD_DIM)
        q4 = jnp.concatenate(
            [q[:, (kh * SWA_GROUP + g) * HEAD_DIM:(kh * SWA_GROUP + g + 1) * HEAD_DIM]
             for g in range(SWA_GROUP)], axis=0).astype(BF16)
        slope = _per_group(rb, [_alibi_slope(kh * SWA_GROUP + g, n_heads) for g in range(SWA_GROUP)])
        sink = _per_group(rb, [sink_ref[kh * SWA_GROUP + g] for g in range(SWA_GROUP)])
        acc = jnp.zeros((rows, HEAD_DIM), F32)
        for b in range(bb):
            g8 = 8 * (b // 2)
            head_rows = pl.ds(kh, w, stride=kv_heads)
            kk = jnp.concatenate([kb_ref[b, head_rows, :], kn[g8:g8 + 8, hs]], axis=0).astype(BF16)
            vv = jnp.concatenate([vb_ref[b, head_rows, :], vn[g8:g8 + 8, hs]], axis=0).astype(BF16)
            jj = ci - w
            new_tok = jj % 4
            dist = jnp.where(ci < w, w + t - ci, t - new_tok)
            valid = ((ci < w) & (ci >= t)) | ((ci >= w) & ((jj // 4) == (b % 2)) & (new_tok <= t))
            s = _dot_nt(q4, kk) * (HEAD_DIM ** -0.5) - slope * dist.astype(F32)
            o = _sink_softmax_pv(s, valid, sink, vv)
            acc = jnp.where(row_batch == b, o, acc)
        for g in range(SWA_GROUP):
            c0 = (kh * SWA_GROUP + g) * HEAD_DIM
            o_ref[:, c0:c0 + HEAD_DIM] = acc[g * rb:(g + 1) * rb].astype(BF16)
    keep = (w - 4) * kv_heads
    for b in range(bb):
        nk_ref[b, 0:keep, :] = kb_ref[b, 4 * kv_heads:w * kv_heads, :]
        nv_ref[b, 0:keep, :] = vb_ref[b, 4 * kv_heads:w * kv_heads, :]
        for tok in range(4):
            for kh in range(kv_heads):
                hs = slice(kh * HEAD_DIM, (kh + 1) * HEAD_DIM)
                dst = pl.ds(keep + tok * kv_heads + kh, 1)
                nk_ref[b, dst, :] = kn_ref[pl.ds(4 * b + tok, 1), hs]
                nv_ref[b, dst, :] = vn_ref[pl.ds(4 * b + tok, 1), hs]


def _swa_sample(proj_s, col0, sinks, k_buf, v_buf, o_swa, p, kv_heads):
    m = proj_s.shape[0]
    db, wrows, hd = k_buf.shape
    assert wrows == WINDOW * kv_heads and hd == HEAD_DIM
    qw = kv_heads * SWA_GROUP * HEAD_DIM
    kvw = kv_heads * HEAD_DIM
    qcol, kcol, vcol = _swa_cols(col0, qw, kvw)
    bb = SWA_S_STEP_BATCHES
    rb = 4 * bb
    pb = p // rb
    return pl.pallas_call(
        functools.partial(_swa_sample_body, kv_heads=kv_heads),
        grid=(db // bb,),
        in_specs=[pl.BlockSpec(memory_space=pltpu.SMEM),
                  pl.BlockSpec((rb, qw), lambda i: (pb + i, qcol)),
                  pl.BlockSpec((rb, kvw), lambda i: (pb + i, kcol)),
                  pl.BlockSpec((rb, kvw), lambda i: (pb + i, vcol)),
                  pl.BlockSpec((bb, wrows, hd), lambda i: (i, 0, 0)),
                  pl.BlockSpec((bb, wrows, hd), lambda i: (i, 0, 0)),
                  pl.BlockSpec(memory_space=pl.ANY)],
        out_specs=[pl.BlockSpec((rb, qw), lambda i: (pb + i, 0)),
                   pl.BlockSpec((bb, wrows, hd), lambda i: (i, 0, 0)),
                   pl.BlockSpec((bb, wrows, hd), lambda i: (i, 0, 0))],
        out_shape=[jax.ShapeDtypeStruct((m, qw), BF16),
                   jax.ShapeDtypeStruct(k_buf.shape, F32),
                   jax.ShapeDtypeStruct(v_buf.shape, F32)],
        input_output_aliases={6: 0},
        compiler_params=_params(1),
        name="swa_sample",
    )(sinks, proj_s, proj_s, proj_s, k_buf, v_buf, o_swa)


def _ffn_up_body(a_ref, wg_ref, wv_ref, cwg_ref, cwv_ref, cbg_ref, cbv_ref,
                 c0g_ref, c0v_ref, c1g_ref, c1v_ref,
                 o_ref, cpg_ref, cpv_ref, n2g_ref, n2v_ref, n3g_ref, n3v_ref,
                 wbf, carry, prev1_s, prev2_s, u_s):
    j = pl.program_id(0)
    i = pl.program_id(1)
    n_i = pl.num_programs(1)
    tm, tn = o_ref.shape
    ts = u_s.shape[1]
    db = ts // 4

    @pl.when((i == 0) & (j == 0))
    def _():
        prev1_s[...] = jnp.zeros_like(prev1_s)
        prev2_s[...] = jnp.zeros_like(prev2_s)

    @pl.when(i == 0)
    def _():
        wbf[:, :tn] = wg_ref[...].astype(BF16)
        wbf[:, tn:] = wv_ref[...].astype(BF16)
        carry[...] = jnp.zeros_like(carry)

    cw = jnp.concatenate([cwg_ref[...], cwv_ref[...]], axis=1)
    cb = jnp.concatenate([cbg_ref[...], cbv_ref[...]], axis=1)

    def conv_gate(u, prev2, prev1):
        uc = cb + cw[0:1] * prev2 + cw[1:2] * prev1 + cw[2:3] * u
        return (_silu(uc[:, :tn]) * uc[:, tn:]).astype(BF16)

    @pl.when(i < n_i - 1)
    def _():
        u = _dot(a_ref[...], wbf[...])
        row = lax.broadcasted_iota(jnp.int32, (tm, 1), 0)
        prev1 = jnp.where(row == 0, carry[7:8, :], pltpu.roll(u, 1, 0))
        prev2 = jnp.where(row == 0, carry[6:7, :],
                          jnp.where(row == 1, carry[7:8, :], pltpu.roll(u, 2, 0)))
        o_ref[...] = conv_gate(u, prev2, prev1)
        carry[...] = u[tm - 8:tm, :]

        @pl.when(i == n_i - 2)
        def _():
            cpg_ref[...] = u[tm - 8:tm, :tn]
            cpv_ref[...] = u[tm - 8:tm, tn:]

    @pl.when(i == n_i - 1)
    def _():
        u = _dot(a_ref[0:ts, :], wbf[...])
        c0 = jnp.concatenate([c0g_ref[...], c0v_ref[...]], axis=1)
        c1 = jnp.concatenate([c1g_ref[...], c1v_ref[...]], axis=1)
        first = pl.ds(0, db, stride=4)
        second = pl.ds(1, db, stride=4)
        lane_tiles = range(2 * tn // LANES)
        for c in lane_tiles:
            ls = slice(c * LANES, (c + 1) * LANES)
            prev2_s[c, first, :] = c0[:, ls]
            prev2_s[c, second, :] = c1[:, ls]
            prev1_s[c, first, :] = c1[:, ls]
            u_s[c] = u[:, ls]
        tok = lax.broadcasted_iota(jnp.int32, (ts, 1), 0) % 4
        patch1 = jnp.concatenate([prev1_s[c] for c in lane_tiles], axis=1)
        patch2 = jnp.concatenate([prev2_s[c] for c in lane_tiles], axis=1)
        prev1 = jnp.where(tok == 0, patch1, pltpu.roll(u, 1, 0))
        prev2 = jnp.where(tok < 2, patch2, pltpu.roll(u, 2, 0))
        o_ref[0:ts, :] = conv_gate(u, prev2, prev1)
        for t, (g_ref, v_ref) in enumerate(((n2g_ref, n2v_ref), (n3g_ref, n3v_ref))):
            rows = pl.ds(2 + t, db, stride=4)
            new = jnp.concatenate([u_s[c, rows, :] for c in lane_tiles], axis=1)
            g_ref[...] = new[:, :tn]
            v_ref[...] = new[:, tn:]


def _ffn_up(h2, w_up, w_conv, b_conv, conv_buf, p, tm, tn=256):
    m, d = h2.shape
    f2 = w_up.shape[1]
    dff = f2 // 2
    ts = m - p
    db = ts // 4
    assert p % tm == 0 and ts <= tm and dff % tn == 0
    nj = dff // tn
    bc = b_conv.reshape(1, f2)
    cbuf = conv_buf.reshape(db, 2 * f2)
    col = lambda k: (lambda j, i: (0, j + k * nj))
    gate, val = col(0), col(1)
    lane_tiles = 2 * tn // LANES
    dstate = jax.ShapeDtypeStruct((db, dff), F32)
    return pl.pallas_call(
        _ffn_up_body,
        grid=(nj, p // tm + 1),
        in_specs=[pl.BlockSpec((tm, d), lambda j, i: (i, 0)),
                  pl.BlockSpec((d, tn), gate), pl.BlockSpec((d, tn), val),
                  pl.BlockSpec((3, tn), gate), pl.BlockSpec((3, tn), val),
                  pl.BlockSpec((1, tn), gate), pl.BlockSpec((1, tn), val)]
        + [pl.BlockSpec((db, tn), col(k)) for k in range(4)],
        out_specs=[pl.BlockSpec((tm, tn), lambda j, i: (i, j)),
                   pl.BlockSpec((8, tn), gate), pl.BlockSpec((8, tn), gate)]
        + [pl.BlockSpec((db, tn), gate) for _ in range(4)],
        out_shape=[jax.ShapeDtypeStruct((m, dff), BF16),
                   jax.ShapeDtypeStruct((8, dff), F32), jax.ShapeDtypeStruct((8, dff), F32),
                   dstate, dstate, dstate, dstate],
        scratch_shapes=[pltpu.VMEM((d, 2 * tn), BF16), pltpu.VMEM((8, 2 * tn), F32)]
        + [pltpu.VMEM((lane_tiles, ts, LANES), F32) for _ in range(3)],
        compiler_params=_params(2),
        name="ffn_up_conv_gate",
    )(h2, w_up, w_up, w_conv, w_conv, bc, bc, cbuf, cbuf, cbuf, cbuf)


def _pick_tile(m, candidates):
    for c in candidates:
        if m % c == 0:
            return c
    return m


def _layer(x_p, x_s, c_p, c_s, s_gla, k_buf, v_buf, conv_buf, w_ada, b_ada, g_norm, w_in,
           w_a_up, b_a, g_gla, sinks, w_o, w_up, w_conv, b_conv, w_down, g_final):
    p, d = x_p.shape
    db = c_s.shape[0]
    m = p + 4 * db
    heads, dk, dv = s_gla.shape[1:]
    qk, gw = heads * dk, heads * dv
    kv_heads = k_buf.shape[2]
    kvw = kv_heads * HEAD_DIM
    qw = kv_heads * SWA_GROUP * HEAD_DIM
    lr = w_a_up.shape[0]
    f2 = w_up.shape[1]

    c_all = jnp.concatenate([c_s, c_p, jnp.zeros((7, d), F32)], axis=0)
    mod = _ada(c_all, w_ada, b_ada)

    h = _prenorm(x_p, x_s, mod, g_norm[0:1])

    tm = _pick_tile(m, (1088, 512, 384, 128))
    n_g = 2 * qk + 2 * gw
    n_s = qw + 2 * kvw
    tn = 512
    w_t = jnp.transpose(w_in)
    proj = _mm_nt(h, w_t, lambda j: j * tn + jnp.where(j >= n_g // tn, lr, 0), (n_g + n_s) // tn,
                  tm, tn, F32, "in_proj")
    proj_a = _mm_nt(h, w_t, lambda j: j + n_g, 1, tm, LANES, F32, "in_proj_decay")

    wa = jnp.concatenate([w_a_up, jnp.zeros((LANES - lr, qk), F32)], axis=0)
    ba = b_a.reshape(1, qk)
    gg = g_gla.reshape(1, dv)
    o_gla, s_p = _gla_prompt(proj, proj_a, wa, ba, gg, p, heads, dk, dv)
    o_gla, s_s = _gla_sample(proj, proj_a, wa, ba, gg, s_gla, o_gla, p, heads, dk, dv)

    o_swa = _swa_prompt(proj, n_g, sinks, p, kv_heads)
    win_rows = WINDOW * kv_heads
    o_swa, nk_s, nv_s = _swa_sample(proj, n_g, sinks, k_buf.reshape(db, win_rows, HEAD_DIM),
                                    v_buf.reshape(db, win_rows, HEAD_DIM), o_swa, p, kv_heads)
    nk_p = proj[p - WINDOW:p, n_g + qw:n_g + qw + kvw]
    nv_p = proj[p - WINDOW:p, n_g + qw + kvw:n_g + n_s]

    attn = _mm([o_gla, o_swa], w_o, d, tm, 512, F32, "out_proj")
    x1, h2 = _postattn(x_p, x_s, attn, mod, g_norm[1:2])

    g_act, cpg, cpv, n2g, n2v, n3g, n3v = _ffn_up(h2, w_up, w_conv, b_conv, conv_buf, p, min(1024, p))
    conv_p = jnp.concatenate([cpg[6:8], cpv[6:8]], axis=1)
    conv_s = jnp.stack([jnp.concatenate([n2g, n2v], axis=1),
                        jnp.concatenate([n3g, n3v], axis=1)], axis=1)

    down = _mm([g_act], w_down, d, _pick_tile(m, (512, 384, 128)), 256, F32, "ffn_down")
    y_p, y_s = _final(x1, down, mod, g_final.reshape(1, d), p)
    return (y_p, y_s, s_p, nk_p.reshape(WINDOW, kv_heads, HEAD_DIM),
            nv_p.reshape(WINDOW, kv_heads, HEAD_DIM), conv_p,
            s_s, nk_s.reshape(db, WINDOW, kv_heads, HEAD_DIM),
            nv_s.reshape(db, WINDOW, kv_heads, HEAD_DIM), conv_s)


def kernel(x_prompt, x_sample, c_prompt, c_sample, state_gla, state_swa_k, state_swa_v, state_ffn_conv, w_ada, b_ada, g_norm, w_in, w_a_up, b_a, g_gla, swa_sinks, w_o, w_up, w_conv, b_conv, w_down, g_final):
    assert x_prompt.shape[0] == 1 and w_ada.shape[0] == 1
    db, ds, d = x_sample.shape
    assert ds == 4
    outs = _layer(x_prompt[0], x_sample.reshape(db * ds, d), c_prompt, c_sample,
                  state_gla[0], state_swa_k[0], state_swa_v[0], state_ffn_conv[0],
                  w_ada[0], b_ada[0], g_norm[0], w_in[0], w_a_up[0], b_a[0], g_gla[0],
                  swa_sinks[0], w_o[0], w_up[0], w_conv[0], b_conv[0], w_down[0], g_final)
    y_p, y_s, s_p, nk_p, nv_p, conv_p, s_s, nk_s, nv_s, conv_s = outs
    return (y_p[None], y_s.reshape(db, ds, d), s_p[None, None], nk_p[None, None], nv_p[None, None],
            conv_p[None, None], s_s[None], nk_s[None], nv_s[None], conv_s[None])
```

```python
import functools

import jax
import jax.numpy as jnp
from jax import lax
from jax.experimental import pallas as pl
from jax.experimental.pallas import tpu as pltpu

F32 = jnp.float32
BF16 = jnp.bfloat16

GLA_TAU = 16.0
GLA_CHUNK = 64
HEAD_DIM = 128
SWA_GROUP = 4
WINDOW = 128
NORM_EPS = 1e-6
NEG_INF = -1e30

LANES = 128
ROW_BLOCK = 128
VMEM_LIMIT_BYTES = 56 * 1024 * 1024


def _params(n_axes):
    return pltpu.CompilerParams(
        dimension_semantics=("arbitrary",) * n_axes,
        vmem_limit_bytes=VMEM_LIMIT_BYTES)


def _silu(x):
    return x * (1.0 / (1.0 + jnp.exp(-x)))


def _log_sigmoid(z):
    return jnp.minimum(z, 0.0) - jnp.log(1.0 + jnp.exp(-jnp.abs(z)))


def _rms(x, g):
    return x * lax.rsqrt(jnp.mean(x * x, axis=-1, keepdims=True) + NORM_EPS) * g


def _dot(a, b):
    return jnp.dot(a, b, preferred_element_type=F32)


def _dot_nt(a, b):
    return lax.dot_general(a, b, (((1,), (1,)), ((), ())), preferred_element_type=F32)


def _dot_exact_lhs(t_bf16, x):
    hi = x.astype(BF16)
    r1 = x - hi.astype(F32)
    mid = r1.astype(BF16)
    lo = (r1 - mid.astype(F32)).astype(BF16)
    return _dot(t_bf16, hi) + _dot(t_bf16, mid) + _dot(t_bf16, lo)


def _ada_body(c_ref, w_ref, b_ref, o_ref):
    a = _silu(c_ref[...]).astype(BF16)
    o_ref[...] = _dot(a, w_ref[...].astype(BF16)) + b_ref[...]


def _ada(c_all, w_ada, b_ada, tn=512):
    rows, d = c_all.shape
    n = w_ada.shape[1]
    return pl.pallas_call(
        _ada_body,
        grid=(n // tn,),
        in_specs=[pl.BlockSpec((rows, d), lambda j: (0, 0)),
                  pl.BlockSpec((d, tn), lambda j: (0, j)),
                  pl.BlockSpec((1, tn), lambda j: (0, j))],
        out_specs=pl.BlockSpec((rows, tn), lambda j: (0, j)),
        out_shape=jax.ShapeDtypeStruct((rows, n), F32),
        compiler_params=_params(1),
        name="ada_mod",
    )(c_all, w_ada, b_ada.reshape(1, n))


def _p_spec(width, n_p, col=0):
    return pl.BlockSpec((ROW_BLOCK, width), lambda i: (jnp.minimum(i, n_p - 1), col))


def _s_spec(width, n_p, col=0):
    return pl.BlockSpec((ROW_BLOCK, width), lambda i: (jnp.maximum(i - n_p, 0), col))


def _modp_spec(width, col, prompt_row_block):
    return pl.BlockSpec((8, width), lambda i: (prompt_row_block, col))


def _mods_spec(width, n_p, col):
    return pl.BlockSpec((ROW_BLOCK // 4, width), lambda i: (jnp.maximum(i - n_p, 0), col))


def _all_spec(width):
    return pl.BlockSpec((ROW_BLOCK, width), lambda i: (i, 0))


def _per_token(ref):
    v = ref[...]
    n = v.shape[0]
    tok_batch = lax.broadcasted_iota(jnp.int32, (4 * n, n), 0) // 4
    batch = lax.broadcasted_iota(jnp.int32, (4 * n, n), 1)
    return _dot_exact_lhs((tok_batch == batch).astype(BF16), v)


def _prenorm_body(xp, xs, scp, shp, scs, shs, g, o, *, n_p):
    i = pl.program_id(0)

    @pl.when(i < n_p)
    def _():
        h = _rms(xp[...], g[...]) * (1.0 + scp[0:1, :]) + shp[0:1, :]
        o[...] = h.astype(BF16)

    @pl.when(i >= n_p)
    def _():
        h = _rms(xs[...], g[...]) * (1.0 + _per_token(scs)) + _per_token(shs)
        o[...] = h.astype(BF16)


def _prenorm(x_p, x_s, mod, g):
    p, d = x_p.shape
    m = p + x_s.shape[0]
    n_p = p // ROW_BLOCK
    prow = (m - p) // 4 // 8
    return pl.pallas_call(
        functools.partial(_prenorm_body, n_p=n_p),
        grid=(m // ROW_BLOCK,),
        in_specs=[_p_spec(d, n_p), _s_spec(d, n_p),
                  _modp_spec(d, 1, prow), _modp_spec(d, 0, prow),
                  _mods_spec(d, n_p, 1), _mods_spec(d, n_p, 0),
                  pl.BlockSpec((1, d), lambda i: (0, 0))],
        out_specs=_all_spec(d),
        out_shape=jax.ShapeDtypeStruct((m, d), BF16),
        compiler_params=_params(1),
        name="prenorm1",
    )(x_p, x_s, mod, mod, mod, mod, g)


def _postattn_body(xp, xs, at, gtp, scp, shp, gts, scs, shs, g, x1, h2, *, n_p):
    i = pl.program_id(0)

    @pl.when(i < n_p)
    def _():
        x = xp[...] + gtp[0:1, :] * at[...]
        x1[...] = x
        h2[...] = (_rms(x, g[...]) * (1.0 + scp[0:1, :]) + shp[0:1, :]).astype(BF16)

    @pl.when(i >= n_p)
    def _():
        x = xs[...] + _per_token(gts) * at[...]
        x1[...] = x
        h2[...] = (_rms(x, g[...]) * (1.0 + _per_token(scs)) + _per_token(shs)).astype(BF16)


def _postattn(x_p, x_s, attn, mod, g):
    p, d = x_p.shape
    m = p + x_s.shape[0]
    n_p = p // ROW_BLOCK
    prow = (m - p) // 4 // 8
    return pl.pallas_call(
        functools.partial(_postattn_body, n_p=n_p),
        grid=(m // ROW_BLOCK,),
        in_specs=[_p_spec(d, n_p), _s_spec(d, n_p), _all_spec(d),
                  _modp_spec(d, 2, prow), _modp_spec(d, 4, prow), _modp_spec(d, 3, prow),
                  _mods_spec(d, n_p, 2), _mods_spec(d, n_p, 4), _mods_spec(d, n_p, 3),
                  pl.BlockSpec((1, d), lambda i: (0, 0))],
        out_specs=[_all_spec(d), _all_spec(d)],
        out_shape=[jax.ShapeDtypeStruct((m, d), F32), jax.ShapeDtypeStruct((m, d), BF16)],
        compiler_params=_params(1),
        name="postattn",
    )(x_p, x_s, attn, mod, mod, mod, mod, mod, mod, g)


def _final_body(x1, dn, gtp, gts, g, yp, ys, *, n_p):
    i = pl.program_id(0)

    @pl.when(i < n_p)
    def _():
        yp[...] = _rms(x1[...] + gtp[0:1, :] * dn[...], g[...])

    @pl.when(i >= n_p)
    def _():
        ys[...] = _rms(x1[...] + _per_token(gts) * dn[...], g[...])


def _final(x1, down, mod, g, p):
    m, d = x1.shape
    n_p = p // ROW_BLOCK
    prow = (m - p) // 4 // 8
    return pl.pallas_call(
        functools.partial(_final_body, n_p=n_p),
        grid=(m // ROW_BLOCK,),
        in_specs=[_all_spec(d), _all_spec(d), _modp_spec(d, 5, prow), _mods_spec(d, n_p, 5),
                  pl.BlockSpec((1, d), lambda i: (0, 0))],
        out_specs=[_p_spec(d, n_p), _s_spec(d, n_p)],
        out_shape=[jax.ShapeDtypeStruct((p, d), F32), jax.ShapeDtypeStruct((m - p, d), F32)],
        compiler_params=_params(1),
        name="final_norm",
    )(x1, down, mod, mod, g)


def _mm_body(*refs, k_splits):
    n_a = len(k_splits)
    a_refs, w_ref, o_ref, wbf = refs[:n_a], refs[n_a], refs[n_a + 1], refs[n_a + 2]

    @pl.when(pl.program_id(1) == 0)
    def _():
        wbf[...] = w_ref[...].astype(BF16)

    acc = None
    k0 = 0
    for a_ref, kk in zip(a_refs, k_splits):
        part = _dot(a_ref[...], wbf[k0:k0 + kk, :])
        acc = part if acc is None else acc + part
        k0 += kk
    o_ref[...] = acc.astype(o_ref.dtype)


def _mm(a_list, w, n_cols, tm, tn, out_dtype, name, col_block0=0):
    m = a_list[0].shape[0]
    k_splits = tuple(a.shape[1] for a in a_list)
    k = sum(k_splits)
    assert w.shape[0] == k and m % tm == 0 and n_cols % tn == 0
    return pl.pallas_call(
        functools.partial(_mm_body, k_splits=k_splits),
        grid=(n_cols // tn, m // tm),
        in_specs=[pl.BlockSpec((tm, kk), lambda j, i: (i, 0)) for kk in k_splits]
        + [pl.BlockSpec((k, tn), lambda j, i: (0, j + col_block0))],
        out_specs=pl.BlockSpec((tm, tn), lambda j, i: (i, j)),
        out_shape=jax.ShapeDtypeStruct((m, n_cols), out_dtype),
        scratch_shapes=[pltpu.VMEM((k, tn), BF16)],
        compiler_params=_params(2),
        name=name,
    )(*a_list, w)


def _mm_nt_body(a_ref, w_ref, o_ref, wbf):
    @pl.when(pl.program_id(1) == 0)
    def _():
        wbf[...] = w_ref[...].astype(BF16)

    o_ref[...] = _dot_nt(a_ref[...], wbf[...]).astype(o_ref.dtype)


def _mm_nt(a, w_t, row_offset, n_tiles, tm, tn, out_dtype, name):
    m, k = a.shape
    assert w_t.shape[1] == k and m % tm == 0
    return pl.pallas_call(
        _mm_nt_body,
        grid=(n_tiles, m // tm),
        in_specs=[pl.BlockSpec((tm, k), lambda j, i: (i, 0)),
                  pl.BlockSpec((pl.Element(tn), pl.Element(k)),
                               lambda j, i: (pl.multiple_of(row_offset(j), 8), 0))],
        out_specs=pl.BlockSpec((tm, tn), lambda j, i: (i, j)),
        out_shape=jax.ShapeDtypeStruct((m, n_tiles * tn), out_dtype),
        scratch_shapes=[pltpu.VMEM((tn, k), BF16)],
        compiler_params=_params(2),
        name=name,
    )(a, w_t)


def _gla_gates(ga, wa, ba):
    z = _dot(ga.astype(BF16), wa.astype(BF16)) + ba
    return _log_sigmoid(z) * (1.0 / GLA_TAU)


def _gla_out(o, gg, r):
    return (_rms(o, gg) * _silu(r)).astype(BF16)


def _gla_prompt_body(q_ref, k_ref, v_ref, r_ref, ga_ref, wa_ref, ba_ref, gg_ref,
                     o_ref, sf_ref, s_scr, *, heads, dk, dv):
    c = pl.program_id(0)
    ch = q_ref.shape[0]

    @pl.when(c == 0)
    def _():
        s_scr[...] = jnp.zeros_like(s_scr)

    la = _gla_gates(ga_ref[...], wa_ref[...], ba_ref[...])
    row = lax.broadcasted_iota(jnp.int32, (ch, ch), 0)
    col = lax.broadcasted_iota(jnp.int32, (ch, ch), 1)
    causal = row >= col
    b = _dot_exact_lhs(causal.astype(BF16), la)
    b_last = b[ch - 1:ch, :]
    scale = dk ** -0.5
    qg = (q_ref[...] * scale) * jnp.exp(b)
    kg = k_ref[...] * jnp.exp(-b)
    kd = k_ref[...] * jnp.exp(b_last - b)
    pad = jnp.zeros((LANES - ch, la.shape[1]), F32) if ch < LANES else None
    la_t = jnp.transpose(la if pad is None else jnp.concatenate([la, pad], axis=0))
    decay_col = jnp.exp(jnp.sum(la_t, axis=1, keepdims=True))
    v = v_ref[...]
    r = r_ref[...]
    for h in range(heads):
        ks, vs = slice(h * dk, (h + 1) * dk), slice(h * dv, (h + 1) * dv)
        qh = qg[:, ks].astype(BF16)
        vh = v[:, vs].astype(BF16)
        a = jnp.where(causal, _dot_nt(qh, kg[:, ks].astype(BF16)), 0.0)
        s0 = s_scr[h]
        o = _dot(a.astype(BF16), vh) + _dot(qh, s0.astype(BF16))
        kdh = kd[:, ks]
        if pad is not None:
            kdh = jnp.concatenate([kdh, pad[:, :dk]], axis=0)
            vh = jnp.concatenate([vh, jnp.zeros((LANES - ch, dv), BF16)], axis=0)
        s_scr[h] = decay_col[ks, :] * s0 + _dot(jnp.transpose(kdh).astype(BF16), vh)
        o_ref[:, vs] = _gla_out(o, gg_ref[...], r[:, vs])

    @pl.when(c == pl.num_programs(0) - 1)
    def _():
        sf_ref[...] = s_scr[...]


def _gla_prompt(proj_g, proj_s, wa, ba, gg, p, heads, dk, dv):
    m = proj_g.shape[0]
    ch = GLA_CHUNK
    qk, gw = heads * dk, heads * dv
    ga_col = (proj_s.shape[1] - LANES) // LANES
    return pl.pallas_call(
        functools.partial(_gla_prompt_body, heads=heads, dk=dk, dv=dv),
        grid=(p // ch,),
        in_specs=[pl.BlockSpec((ch, qk), lambda c: (c, 0)),
                  pl.BlockSpec((ch, qk), lambda c: (c, 1)),
                  pl.BlockSpec((ch, gw), lambda c: (c, 1)),
                  pl.BlockSpec((ch, gw), lambda c: (c, 2)),
                  pl.BlockSpec((ch, LANES), lambda c: (c, ga_col)),
                  pl.BlockSpec((LANES, qk), lambda c: (0, 0)),
                  pl.BlockSpec((1, qk), lambda c: (0, 0)),
                  pl.BlockSpec((1, dv), lambda c: (0, 0))],
        out_specs=[pl.BlockSpec((ch, gw), lambda c: (c, 0)),
                   pl.BlockSpec((heads, dk, dv), lambda c: (0, 0, 0))],
        out_shape=[jax.ShapeDtypeStruct((m, gw), BF16),
                   jax.ShapeDtypeStruct((heads, dk, dv), F32)],
        scratch_shapes=[pltpu.VMEM((heads, dk, dv), F32)],
        compiler_params=_params(1),
        name="gla_prompt",
    )(proj_g, proj_g, proj_g, proj_g, proj_s, wa, ba, gg)


GLA_S_TILE_BATCHES = ROW_BLOCK // 4
GLA_S_STEP_BATCHES = 2


def _gla_sample_body(q_ref, k_ref, v_ref, r_ref, ga_ref, wa_ref, ba_ref, gg_ref, st_ref, _o_in,
                     o_ref, ns_ref, qg_s, kdt_s, v_s, lat_s, o_s, *, heads, dk, dv):
    s = pl.program_id(1)
    n_sub = pl.num_programs(1)
    rows = ROW_BLOCK

    @pl.when(s == 0)
    def _():
        la = _gla_gates(ga_ref[...], wa_ref[...], ba_ref[...])
        row = lax.broadcasted_iota(jnp.int32, (rows, rows), 0)
        col = lax.broadcasted_iota(jnp.int32, (rows, rows), 1)
        same = (row // 4) == (col // 4)
        causal = same & (row >= col)
        b = _dot_exact_lhs(causal.astype(BF16), la)
        b_last = _dot_exact_lhs(same.astype(BF16), la)
        scale = dk ** -0.5
        qg = (q_ref[...] * scale) * jnp.exp(b)
        kg = k_ref[...] * jnp.exp(-b)
        kd = k_ref[...] * jnp.exp(b_last - b)
        lat_s[...] = jnp.transpose(la)
        v = v_ref[...].astype(BF16)
        v_s[...] = v
        qg_s[...] = qg.astype(BF16)
        for h in range(heads):
            ks, vs = slice(h * dk, (h + 1) * dk), slice(h * dv, (h + 1) * dv)
            a = jnp.where(causal, _dot_nt(qg[:, ks].astype(BF16), kg[:, ks].astype(BF16)), 0.0)
            o_s[:, vs] = _dot(a.astype(BF16), v[:, vs])
            kdt_s[h] = jnp.transpose(kd[:, ks]).astype(BF16)

    lane = lax.broadcasted_iota(jnp.int32, (1, rows), 1)
    sub = lax.broadcasted_iota(jnp.int32, (16, 1), 0)
    for i in range(GLA_S_STEP_BATCHES):
        bl = s * GLA_S_STEP_BATCHES + i
        lane_mask = (lane // 4) == bl
        decay_col = jnp.exp(jnp.sum(jnp.where(lane_mask, lat_s[...], 0.0), axis=1, keepdims=True))
        slab = pl.ds(pl.multiple_of((bl // 4) * 16, 16), 16)
        row_mask = (sub // 4) == (bl % 4)
        for h in range(heads):
            ks, vs = slice(h * dk, (h + 1) * dk), slice(h * dv, (h + 1) * dv)
            s0 = st_ref[i, h]
            oi = _dot(qg_s[slab, ks], s0.astype(BF16))
            o_s[slab, vs] = o_s[slab, vs] + jnp.where(row_mask, oi, 0.0)
            kdt = jnp.where(lane_mask, kdt_s[h], jnp.zeros_like(kdt_s[h]))
            ns_ref[i, h] = decay_col[ks, :] * s0 + _dot(kdt, v_s[:, vs])

    @pl.when(s == n_sub - 1)
    def _():
        r = r_ref[...]
        for h in range(heads):
            vs = slice(h * dv, (h + 1) * dv)
            o_ref[:, vs] = _gla_out(o_s[:, vs], gg_ref[...], r[:, vs])


def _gla_sample(proj_g, proj_s, wa, ba, gg, state, o_gla, p, heads, dk, dv):
    m = proj_g.shape[0]
    db = state.shape[0]
    qk, gw = heads * dk, heads * dv
    ga_col = (proj_s.shape[1] - LANES) // LANES
    pb = p // ROW_BLOCK
    n_tiles = db // GLA_S_TILE_BATCHES
    n_sub = GLA_S_TILE_BATCHES // GLA_S_STEP_BATCHES
    sb = GLA_S_STEP_BATCHES
    return pl.pallas_call(
        functools.partial(_gla_sample_body, heads=heads, dk=dk, dv=dv),
        grid=(n_tiles, n_sub),
        in_specs=[pl.BlockSpec((ROW_BLOCK, qk), lambda t, s: (pb + t, 0)),
                  pl.BlockSpec((ROW_BLOCK, qk), lambda t, s: (pb + t, 1)),
                  pl.BlockSpec((ROW_BLOCK, gw), lambda t, s: (pb + t, 1)),
                  pl.BlockSpec((ROW_BLOCK, gw), lambda t, s: (pb + t, 2)),
                  pl.BlockSpec((ROW_BLOCK, LANES), lambda t, s: (pb + t, ga_col)),
                  pl.BlockSpec((LANES, qk), lambda t, s: (0, 0)),
                  pl.BlockSpec((1, qk), lambda t, s: (0, 0)),
                  pl.BlockSpec((1, dv), lambda t, s: (0, 0)),
                  pl.BlockSpec((sb, heads, dk, dv), lambda t, s: (t * n_sub + s, 0, 0, 0)),
                  pl.BlockSpec(memory_space=pl.ANY)],
        out_specs=[pl.BlockSpec((ROW_BLOCK, gw), lambda t, s: (pb + t, 0)),
                   pl.BlockSpec((sb, heads, dk, dv), lambda t, s: (t * n_sub + s, 0, 0, 0))],
        out_shape=[jax.ShapeDtypeStruct((m, gw), BF16),
                   jax.ShapeDtypeStruct(state.shape, F32)],
        scratch_shapes=[pltpu.VMEM((ROW_BLOCK, qk), BF16),
                        pltpu.VMEM((heads, dk, ROW_BLOCK), BF16),
                        pltpu.VMEM((ROW_BLOCK, gw), BF16),
                        pltpu.VMEM((qk, ROW_BLOCK), F32),
                        pltpu.VMEM((ROW_BLOCK, gw), F32)],
        input_output_aliases={9: 0},
        compiler_params=_params(2),
        name="gla_sample",
    )(proj_g, proj_g, proj_g, proj_g, proj_s, wa, ba, gg, state, o_gla)


def _alibi_slope(head, n_heads):
    return 2.0 ** (-8.0 * (head + 1) / n_heads)


def _per_group(rows_per_group, values):
    n = SWA_GROUP * rows_per_group
    g = lax.broadcasted_iota(jnp.int32, (n, 1), 0) // rows_per_group
    out = jnp.full((n, 1), values[SWA_GROUP - 1], F32)
    for i in range(SWA_GROUP - 2, -1, -1):
        out = jnp.where(g == i, values[i], out)
    return out


def _sink_softmax_pv(s, valid, sink, vv):
    s = jnp.where(valid, s, NEG_INF)
    mx = jnp.maximum(jnp.max(s, axis=1, keepdims=True), sink)
    pr = jnp.exp(s - mx)
    den = jnp.sum(pr, axis=1, keepdims=True) + jnp.exp(sink - mx)
    return _dot(pr.astype(BF16), vv) / den


def _swa_prompt_body(sink_ref, q_ref, kp_ref, kc_ref, vp_ref, vc_ref, o_ref, *, kv_heads):
    n = pl.program_id(0)
    w = WINDOW
    n_heads = kv_heads * SWA_GROUP
    rows = SWA_GROUP * w
    qi = lax.broadcasted_iota(jnp.int32, (rows, 2 * w), 0) % w
    ci = lax.broadcasted_iota(jnp.int32, (rows, 2 * w), 1)
    dist = w + qi - ci
    valid = (dist >= 0) & (dist <= WINDOW) & ((n > 0) | (ci >= w))
    dist_f = dist.astype(F32)
    q = q_ref[...]
    for kh in range(kv_heads):
        hs = slice(kh * HEAD_DIM, (kh + 1) * HEAD_DIM)
        q4 = jnp.concatenate(
            [q[:, (kh * SWA_GROUP + g) * HEAD_DIM:(kh * SWA_GROUP + g + 1) * HEAD_DIM]
             for g in range(SWA_GROUP)], axis=0).astype(BF16)
        kk = jnp.concatenate([kp_ref[:, hs], kc_ref[:, hs]], axis=0).astype(BF16)
        vv = jnp.concatenate([vp_ref[:, hs], vc_ref[:, hs]], axis=0).astype(BF16)
        slope = _per_group(w, [_alibi_slope(kh * SWA_GROUP + g, n_heads) for g in range(SWA_GROUP)])
        sink = _per_group(w, [sink_ref[kh * SWA_GROUP + g] for g in range(SWA_GROUP)])
        s = _dot_nt(q4, kk) * (HEAD_DIM ** -0.5) - slope * dist_f
        o = _sink_softmax_pv(s, valid, sink, vv)
        for g in range(SWA_GROUP):
            c0 = (kh * SWA_GROUP + g) * HEAD_DIM
            o_ref[:, c0:c0 + HEAD_DIM] = o[g * w:(g + 1) * w].astype(BF16)


def _swa_cols(col0, qw, kvw):
    assert col0 % qw == 0 and (col0 + qw) % kvw == 0
    return col0 // qw, (col0 + qw) // kvw, (col0 + qw) // kvw + 1


def _swa_prompt(proj_s, col0, sinks, p, kv_heads):
    m = proj_s.shape[0]
    qw = kv_heads * SWA_GROUP * HEAD_DIM
    kvw = kv_heads * HEAD_DIM
    qcol, kcol, vcol = _swa_cols(col0, qw, kvw)
    prev = lambda n: jnp.maximum(n - 1, 0)
    return pl.pallas_call(
        functools.partial(_swa_prompt_body, kv_heads=kv_heads),
        grid=(p // WINDOW,),
        in_specs=[pl.BlockSpec(memory_space=pltpu.SMEM),
                  pl.BlockSpec((WINDOW, qw), lambda n: (n, qcol)),
                  pl.BlockSpec((WINDOW, kvw), lambda n: (prev(n), kcol)),
                  pl.BlockSpec((WINDOW, kvw), lambda n: (n, kcol)),
                  pl.BlockSpec((WINDOW, kvw), lambda n: (prev(n), vcol)),
                  pl.BlockSpec((WINDOW, kvw), lambda n: (n, vcol))],
        out_specs=pl.BlockSpec((WINDOW, qw), lambda n: (n, 0)),
        out_shape=jax.ShapeDtypeStruct((m, qw), BF16),
        compiler_params=_params(1),
        name="swa_prompt",
    )(sinks, proj_s, proj_s, proj_s, proj_s, proj_s)


SWA_S_STEP_BATCHES = 8


def _swa_sample_body(sink_ref, q_ref, kn_ref, vn_ref, kb_ref, vb_ref, _o_in,
                     o_ref, nk_ref, nv_ref, *, kv_heads):
    bb = SWA_S_STEP_BATCHES
    rb = 4 * bb
    w = WINDOW
    n_heads = kv_heads * SWA_GROUP
    rows = SWA_GROUP * rb
    keys = w + 8
    rr = lax.broadcasted_iota(jnp.int32, (rows, keys), 0) % rb
    ci = lax.broadcasted_iota(jnp.int32, (rows, keys), 1)
    t = rr % 4
    q = q_ref[...]
    kn = kn_ref[...]
    vn = vn_ref[...]
    row_batch = (lax.broadcasted_iota(jnp.int32, (rows, 1), 0) % rb) // 4
    for kh in range(kv_heads):
        hs = slice(kh * HEAD_DIM, (kh + 1) * HEAD_DIM)
        q4 = jnp.concatenate(
            [q[:, (kh * SWA_GROUP + g) * HEAD_DIM:(kh * SWA_GROUP + g + 1) * HEAD_DIM]
             for g in range(SWA_GROUP)], axis=0).astype(BF16)
        slope = _per_group(rb, [_alibi_slope(kh * SWA_GROUP + g, n_heads) for g in range(SWA_GROUP)])
        sink = _per_group(rb, [sink_ref[kh * SWA_GROUP + g] for g in range(SWA_GROUP)])
        acc = jnp.zeros((rows, HEAD_DIM), F32)
        for b in range(bb):
            g8 = 8 * (b // 2)
            head_rows = pl.ds(kh, w, stride=kv_heads)
            kk = jnp.concatenate([kb_ref[b, head_rows, :], kn[g8:g8 + 8, hs]], axis=0).astype(BF16)
            vv = jnp.concatenate([vb_ref[b, head_rows, :], vn[g8:g8 + 8, hs]], axis=0).astype(BF16)
            jj = ci - w
            new_tok = jj % 4
            dist = jnp.where(ci < w, w + t - ci, t - new_tok)
            valid = ((ci < w) & (ci >= t)) | ((ci >= w) & ((jj // 4) == (b % 2)) & (new_tok <= t))
            s = _dot_nt(q4, kk) * (HEAD_DIM ** -0.5) - slope * dist.astype(F32)
            o = _sink_softmax_pv(s, valid, sink, vv)
            acc = jnp.where(row_batch == b, o, acc)
        for g in range(SWA_GROUP):
            c0 = (kh * SWA_GROUP + g) * HEAD_DIM
            o_ref[:, c0:c0 + HEAD_DIM] = acc[g * rb:(g + 1) * rb].astype(BF16)
    keep = (w - 4) * kv_heads
    for b in range(bb):
        nk_ref[b, 0:keep, :] = kb_ref[b, 4 * kv_heads:w * kv_heads, :]
        nv_ref[b, 0:keep, :] = vb_ref[b, 4 * kv_heads:w * kv_heads, :]
        for tok in range(4):
            for kh in range(kv_heads):
                hs = slice(kh * HEAD_DIM, (kh + 1) * HEAD_DIM)
                dst = pl.ds(keep + tok * kv_heads + kh, 1)
                nk_ref[b, dst, :] = kn_ref[pl.ds(4 * b + tok, 1), hs]
                nv_ref[b, dst, :] = vn_ref[pl.ds(4 * b + tok, 1), hs]


def _swa_sample(proj_s, col0, sinks, k_buf, v_buf, o_swa, p, kv_heads):
    m = proj_s.shape[0]
    db, wrows, hd = k_buf.shape
    assert wrows == WINDOW * kv_heads and hd == HEAD_DIM
    qw = kv_heads * SWA_GROUP * HEAD_DIM
    kvw = kv_heads * HEAD_DIM
    qcol, kcol, vcol = _swa_cols(col0, qw, kvw)
    bb = SWA_S_STEP_BATCHES
    rb = 4 * bb
    pb = p // rb
    return pl.pallas_call(
        functools.partial(_swa_sample_body, kv_heads=kv_heads),
        grid=(db // bb,),
        in_specs=[pl.BlockSpec(memory_space=pltpu.SMEM),
                  pl.BlockSpec((rb, qw), lambda i: (pb + i, qcol)),
                  pl.BlockSpec((rb, kvw), lambda i: (pb + i, kcol)),
                  pl.BlockSpec((rb, kvw), lambda i: (pb + i, vcol)),
                  pl.BlockSpec((bb, wrows, hd), lambda i: (i, 0, 0)),
                  pl.BlockSpec((bb, wrows, hd), lambda i: (i, 0, 0)),
                  pl.BlockSpec(memory_space=pl.ANY)],
        out_specs=[pl.BlockSpec((rb, qw), lambda i: (pb + i, 0)),
                   pl.BlockSpec((bb, wrows, hd), lambda i: (i, 0, 0)),
                   pl.BlockSpec((bb, wrows, hd), lambda i: (i, 0, 0))],
        out_shape=[jax.ShapeDtypeStruct((m, qw), BF16),
                   jax.ShapeDtypeStruct(k_buf.shape, F32),
                   jax.ShapeDtypeStruct(v_buf.shape, F32)],
        input_output_aliases={6: 0},
        compiler_params=_params(1),
        name="swa_sample",
    )(sinks, proj_s, proj_s, proj_s, k_buf, v_buf, o_swa)


def _ffn_up_body(a_ref, wg_ref, wv_ref, cwg_ref, cwv_ref, cbg_ref, cbv_ref,
                 c0g_ref, c0v_ref, c1g_ref, c1v_ref,
                 o_ref, cpg_ref, cpv_ref, n2g_ref, n2v_ref, n3g_ref, n3v_ref,
                 wbf, carry, prev1_s, prev2_s, u_s, lag_s):
    j = pl.program_id(0)
    i = pl.program_id(1)
    n_i = pl.num_programs(1)
    tm, tn = o_ref.shape
    ts = u_s.shape[1]
    db = ts // 4

    @pl.when((i == 0) & (j == 0))
    def _():
        prev1_s[...] = jnp.zeros_like(prev1_s)
        prev2_s[...] = jnp.zeros_like(prev2_s)

    @pl.when(i == 0)
    def _():
        wbf[:, :tn] = wg_ref[...].astype(BF16)
        wbf[:, tn:] = wv_ref[...].astype(BF16)
        carry[...] = jnp.zeros_like(carry)

    cw = jnp.concatenate([cwg_ref[...], cwv_ref[...]], axis=1)
    cb = jnp.concatenate([cbg_ref[...], cbv_ref[...]], axis=1)

    def conv_gate(u, prev2, prev1):
        uc = cb + cw[0:1] * prev2 + cw[1:2] * prev1 + cw[2:3] * u
        return (_silu(uc[:, :tn]) * uc[:, tn:]).astype(BF16)

    n_p = n_i - 2

    def finish_tile(u):
        row = lax.broadcasted_iota(jnp.int32, (tm, 1), 0)
        prev1 = jnp.where(row == 0, carry[7:8, :], pltpu.roll(u, 1, 0))
        prev2 = jnp.where(row == 0, carry[6:7, :],
                          jnp.where(row == 1, carry[7:8, :], pltpu.roll(u, 2, 0)))
        o_ref[...] = conv_gate(u, prev2, prev1)
        carry[...] = u[tm - 8:tm, :]

    @pl.when(i == 0)
    def _():
        lag_s[0] = _dot(a_ref[...], wbf[...])

    @pl.when((i >= 1) & (i < n_p))
    def _():
        slot = i % 2
        u_new = _dot(a_ref[...], wbf[...])
        finish_tile(lag_s[1 - slot])
        lag_s[slot] = u_new

    @pl.when(i == n_p)
    def _():
        u = lag_s[(n_p - 1) % 2]
        finish_tile(u)
        cpg_ref[...] = u[tm - 8:tm, :tn]
        cpv_ref[...] = u[tm - 8:tm, tn:]

    @pl.when(i == n_i - 1)
    def _():
        u = _dot(a_ref[0:ts, :], wbf[...])
        c0 = jnp.concatenate([c0g_ref[...], c0v_ref[...]], axis=1)
        c1 = jnp.concatenate([c1g_ref[...], c1v_ref[...]], axis=1)
        first = pl.ds(0, db, stride=4)
        second = pl.ds(1, db, stride=4)
        lane_tiles = range(2 * tn // LANES)
        for c in lane_tiles:
            ls = slice(c * LANES, (c + 1) * LANES)
            prev2_s[c, first, :] = c0[:, ls]
            prev2_s[c, second, :] = c1[:, ls]
            prev1_s[c, first, :] = c1[:, ls]
            u_s[c] = u[:, ls]
        tok = lax.broadcasted_iota(jnp.int32, (ts, 1), 0) % 4
        patch1 = jnp.concatenate([prev1_s[c] for c in lane_tiles], axis=1)
        patch2 = jnp.concatenate([prev2_s[c] for c in lane_tiles], axis=1)
        prev1 = jnp.where(tok == 0, patch1, pltpu.roll(u, 1, 0))
        prev2 = jnp.where(tok < 2, patch2, pltpu.roll(u, 2, 0))
        o_ref[0:ts, :] = conv_gate(u, prev2, prev1)
        for t, (g_ref, v_ref) in enumerate(((n2g_ref, n2v_ref), (n3g_ref, n3v_ref))):
            rows = pl.ds(2 + t, db, stride=4)
            new = jnp.concatenate([u_s[c, rows, :] for c in lane_tiles], axis=1)
            g_ref[...] = new[:, :tn]
            v_ref[...] = new[:, tn:]


def _ffn_up(h2, w_up, w_conv, b_conv, conv_buf, p, tm, tn=256):
    m, d = h2.shape
    f2 = w_up.shape[1]
    dff = f2 // 2
    ts = m - p
    db = ts // 4
    assert p % tm == 0 and ts <= tm and dff % tn == 0
    nj = dff // tn
    bc = b_conv.reshape(1, f2)
    cbuf = conv_buf.reshape(db, 2 * f2)
    col = lambda k: (lambda j, i: (0, j + k * nj))
    gate, val = col(0), col(1)
    lane_tiles = 2 * tn // LANES
    dstate = jax.ShapeDtypeStruct((db, dff), F32)
    n_p = p // tm
    lag_rows = lambda j, i: (jnp.where(i <= n_p, jnp.maximum(i - 1, 0), n_p), j)
    return pl.pallas_call(
        _ffn_up_body,
        grid=(nj, n_p + 2),
        in_specs=[pl.BlockSpec((tm, d), lambda j, i: (jnp.minimum(i, n_p), 0)),
                  pl.BlockSpec((d, tn), gate), pl.BlockSpec((d, tn), val),
                  pl.BlockSpec((3, tn), gate), pl.BlockSpec((3, tn), val),
                  pl.BlockSpec((1, tn), gate), pl.BlockSpec((1, tn), val)]
        + [pl.BlockSpec((db, tn), col(k)) for k in range(4)],
        out_specs=[pl.BlockSpec((tm, tn), lag_rows),
                   pl.BlockSpec((8, tn), gate), pl.BlockSpec((8, tn), gate)]
        + [pl.BlockSpec((db, tn), gate) for _ in range(4)],
        out_shape=[jax.ShapeDtypeStruct((m, dff), BF16),
                   jax.ShapeDtypeStruct((8, dff), F32), jax.ShapeDtypeStruct((8, dff), F32),
                   dstate, dstate, dstate, dstate],
        scratch_shapes=[pltpu.VMEM((d, 2 * tn), BF16), pltpu.VMEM((8, 2 * tn), F32)]
        + [pltpu.VMEM((lane_tiles, ts, LANES), F32) for _ in range(3)]
        + [pltpu.VMEM((2, tm, 2 * tn), F32)],
        compiler_params=_params(2),
        name="ffn_up_conv_gate",
    )(h2, w_up, w_up, w_conv, w_conv, bc, bc, cbuf, cbuf, cbuf, cbuf)


def _pick_tile(m, candidates):
    for c in candidates:
        if m % c == 0:
            return c
    return m


def _layer(x_p, x_s, c_p, c_s, s_gla, k_buf, v_buf, conv_buf, w_ada, b_ada, g_norm, w_in,
           w_a_up, b_a, g_gla, sinks, w_o, w_up, w_conv, b_conv, w_down, g_final):
    p, d = x_p.shape
    db = c_s.shape[0]
    m = p + 4 * db
    heads, dk, dv = s_gla.shape[1:]
    qk, gw = heads * dk, heads * dv
    kv_heads = k_buf.shape[2]
    kvw = kv_heads * HEAD_DIM
    qw = kv_heads * SWA_GROUP * HEAD_DIM
    lr = w_a_up.shape[0]
    f2 = w_up.shape[1]

    c_all = jnp.concatenate([c_s, c_p, jnp.zeros((7, d), F32)], axis=0)
    mod = _ada(c_all, w_ada, b_ada)

    h = _prenorm(x_p, x_s, mod, g_norm[0:1])

    tm = _pick_tile(m, (1088, 512, 384, 128))
    n_g = 2 * qk + 2 * gw
    n_s = qw + 2 * kvw
    tn = 512
    w_t = jnp.transpose(w_in)
    proj = _mm_nt(h, w_t, lambda j: j * tn + jnp.where(j >= n_g // tn, lr, 0), (n_g + n_s) // tn,
                  tm, tn, F32, "in_proj")
    proj_a = _mm_nt(h, w_t, lambda j: j + n_g, 1, tm, LANES, F32, "in_proj_decay")

    wa = jnp.concatenate([w_a_up, jnp.zeros((LANES - lr, qk), F32)], axis=0)
    ba = b_a.reshape(1, qk)
    gg = g_gla.reshape(1, dv)
    o_gla, s_p = _gla_prompt(proj, proj_a, wa, ba, gg, p, heads, dk, dv)
    o_gla, s_s = _gla_sample(proj, proj_a, wa, ba, gg, s_gla, o_gla, p, heads, dk, dv)

    o_swa = _swa_prompt(proj, n_g, sinks, p, kv_heads)
    win_rows = WINDOW * kv_heads
    o_swa, nk_s, nv_s = _swa_sample(proj, n_g, sinks, k_buf.reshape(db, win_rows, HEAD_DIM),
                                    v_buf.reshape(db, win_rows, HEAD_DIM), o_swa, p, kv_heads)
    nk_p = proj[p - WINDOW:p, n_g + qw:n_g + qw + kvw]
    nv_p = proj[p - WINDOW:p, n_g + qw + kvw:n_g + n_s]

    attn = _mm([o_gla, o_swa], w_o, d, tm, 512, F32, "out_proj")
    x1, h2 = _postattn(x_p, x_s, attn, mod, g_norm[1:2])

    g_act, cpg, cpv, n2g, n2v, n3g, n3v = _ffn_up(h2, w_up, w_conv, b_conv, conv_buf, p, min(1024, p))
    conv_p = jnp.concatenate([cpg[6:8], cpv[6:8]], axis=1)
    conv_s = jnp.stack([jnp.concatenate([n2g, n2v], axis=1),
                        jnp.concatenate([n3g, n3v], axis=1)], axis=1)

    down = _mm([g_act], w_down, d, _pick_tile(m, (512, 384, 128)), 256, F32, "ffn_down")
    y_p, y_s = _final(x1, down, mod, g_final.reshape(1, d), p)
    return (y_p, y_s, s_p, nk_p.reshape(WINDOW, kv_heads, HEAD_DIM),
            nv_p.reshape(WINDOW, kv_heads, HEAD_DIM), conv_p,
            s_s, nk_s.reshape(db, WINDOW, kv_heads, HEAD_DIM),
            nv_s.reshape(db, WINDOW, kv_heads, HEAD_DIM), conv_s)


def kernel(x_prompt, x_sample, c_prompt, c_sample, state_gla, state_swa_k, state_swa_v, state_ffn_conv, w_ada, b_ada, g_norm, w_in, w_a_up, b_a, g_gla, swa_sinks, w_o, w_up, w_conv, b_conv, w_down, g_final):
    assert x_prompt.shape[0] == 1 and w_ada.shape[0] == 1
    db, ds, d = x_sample.shape
    assert ds == 4
    outs = _layer(x_prompt[0], x_sample.reshape(db * ds, d), c_prompt, c_sample,
                  state_gla[0], state_swa_k[0], state_swa_v[0], state_ffn_conv[0],
                  w_ada[0], b_ada[0], g_norm[0], w_in[0], w_a_up[0], b_a[0], g_gla[0],
                  swa_sinks[0], w_o[0], w_up[0], w_conv[0], b_conv[0], w_down[0], g_final)
    y_p, y_s, s_p, nk_p, nv_p, conv_p, s_s, nk_s, nv_s, conv_s = outs
    return (y_p[None], y_s.reshape(db, ds, d), s_p[None, None], nk_p[None, None], nv_p[None, None],
            conv_p[None, None], s_s[None], nk_s[None], nv_s[None], conv_s[None])
```
